```python
import jax, jax.numpy as jnp
from jax import lax
import numpy as np

D_MODEL = 1024
BATCH = 4
SEQ = 4096
DEPTH = 4
DEC_BATCH = 128
DEC_SEQ = 8
PAST_LEN = 2048
PAGE_SIZE = 128

N_MIXERS = 2
N_GDN = (DEPTH + 1) // 2
N_NSA = DEPTH // 2
GDN_HEADS = 8
GDN_DK = 128
GDN_DV = 128
GDN_CONV = 4
GDN_CHUNK = 64
GDN_QKV = GDN_HEADS * (2 * GDN_DK + GDN_DV)
GDN_IN = GDN_QKV + GDN_HEADS * GDN_DV + 2 * GDN_HEADS
NSA_HEADS = 16
NSA_KV_GROUPS = 2
NSA_HPG = NSA_HEADS // NSA_KV_GROUPS
NSA_DH = 64
NSA_KV_DIM = NSA_KV_GROUPS * NSA_DH
CMP_LEN = 32
CMP_STRIDE = 16
CMP_HID = 256
SEL_BLOCK = 64
SEL_TOP = 16
WINDOW = 512
Q_BLOCK = 128
NSA_IN = NSA_HEADS * NSA_DH + 6 * NSA_KV_DIM + 3 * NSA_HEADS
D_FF = ((-(-8 * D_MODEL // 3) + 255) // 256) * 256
RMS_EPS = 1e-6
L2_EPS = 1e-6
NEG_BIG = -1e30

kernel_name = 'hybrid_gdn_nsa_decoder_step'


def rmsnorm(x, w):
    xf = x.astype(jnp.float32)
    y = xf * lax.rsqrt(jnp.mean(xf * xf, axis=-1, keepdims=True) + RMS_EPS)
    return (y * w.astype(jnp.float32)).astype(x.dtype)


def l2norm(x):
    xf = x.astype(jnp.float32)
    return xf * lax.rsqrt(jnp.sum(xf * xf, axis=-1, keepdims=True) + L2_EPS)


def masked_softmax(s, mask):
    p = jax.nn.softmax(jnp.where(mask, s, NEG_BIG), axis=-1)
    return jnp.where(mask, p, 0.0)


def swiglu_ffn(x, w_in, w_out):
    g, u = jnp.split(x @ w_in, 2, axis=-1)
    return (jax.nn.silu(g) * u) @ w_out


def causal_short_conv(x, buf, w):
    xp = jnp.concatenate([buf, x], axis=1)
    T = x.shape[1]
    y = xp[:, 0:T] * w[0]
    for i in range(1, GDN_CONV):
        y = y + xp[:, i:i + T] * w[i]
    return jax.nn.silu(y), xp[:, T:]


def gated_delta_chunked(q, k, v, log_a, beta, S0, chunk):
    B, T, H, dk = q.shape
    dv = v.shape[-1]
    n = T // chunk

    def to_chunks(t):
        t = t.reshape((B, n, chunk, H) + t.shape[3:])
        return jnp.moveaxis(t, (1, 3), (0, 2))

    tri = jnp.tril(jnp.ones((chunk, chunk), bool))
    strict = jnp.tril(jnp.ones((chunk, chunk), bool), -1)
    eye = jnp.eye(chunk, dtype=jnp.float32)

    def step(S, inp):
        qc, kc, vc, gc, bc = inp
        G = jnp.cumsum(gc, axis=-1)
        decay = jnp.exp(jnp.where(tri, G[..., :, None] - G[..., None, :], -jnp.inf))
        lmat = jnp.where(strict, bc[..., :, None] * jnp.einsum('bhik,bhjk->bhij', kc, kc) * decay, 0.0)
        eg = jnp.exp(G)
        rhs = bc[..., None] * (vc - eg[..., None] * jnp.einsum('bhck,bhkv->bhcv', kc, S))
        u = lax.linalg.triangular_solve(eye + lmat, rhs, left_side=True, lower=True, unit_diagonal=True)
        o = eg[..., None] * jnp.einsum('bhck,bhkv->bhcv', qc, S) + jnp.einsum(
            'bhij,bhjv->bhiv', jnp.einsum('bhik,bhjk->bhij', qc, kc) * decay, u)
        tail = jnp.exp(G[..., -1:] - G)
        S_new = jnp.exp(G[..., -1])[..., None, None] * S + jnp.einsum('bhck,bhcv->bhkv', kc * tail[..., None], u)
        return S_new, o

    S, o = lax.scan(step, S0, (to_chunks(q), to_chunks(k), to_chunks(v), to_chunks(log_a), to_chunks(beta)))
    o = jnp.moveaxis(o, (0, 2), (1, 3)).reshape(B, T, H, dv)
    return o, S


def gdn_mixer(x, S0, conv_buf, chunk, w_in, conv_w, A_log, dt_bias, norm_w, w_out):
    B, T, _ = x.shape
    h = x @ w_in
    z_end = GDN_QKV + GDN_HEADS * GDN_DV
    qkv, z, b, a = jnp.split(h, [GDN_QKV, z_end, z_end + GDN_HEADS], axis=-1)
    qkv, new_buf = causal_short_conv(qkv, conv_buf, conv_w)
    q, k, v = jnp.split(qkv, [GDN_HEADS * GDN_DK, 2 * GDN_HEADS * GDN_DK], axis=-1)
    q = l2norm(q.reshape(B, T, GDN_HEADS, GDN_DK)) * GDN_DK ** -0.5
    k = l2norm(k.reshape(B, T, GDN_HEADS, GDN_DK))
    v = v.reshape(B, T, GDN_HEADS, GDN_DV).astype(jnp.float32)
    beta = jax.nn.sigmoid(b.astype(jnp.float32))
    log_a = -jnp.exp(A_log.astype(jnp.float32)) * jax.nn.softplus(a.astype(jnp.float32) + dt_bias.astype(jnp.float32))
    o, S = gated_delta_chunked(q, k, v, log_a, beta, S0.astype(jnp.float32), chunk)
    o = rmsnorm(o, norm_w) * jax.nn.silu(z.reshape(B, T, GDN_HEADS, GDN_DV).astype(jnp.float32))
    o = o.reshape(B, T, GDN_HEADS * GDN_DV).astype(x.dtype) @ w_out
    return o, S.astype(S0.dtype), new_buf


def nsa_project(x, w_in):
    B, T, _ = x.shape
    h = x @ w_in
    qd = NSA_HEADS * NSA_DH
    q = h[..., :qd].reshape(B, T, NSA_HEADS, NSA_DH) * NSA_DH ** -0.5
    kv = h[..., qd:qd + 4 * NSA_KV_DIM].reshape(B, T, 4, NSA_KV_GROUPS, NSA_DH)
    win = h[..., qd + 4 * NSA_KV_DIM:qd + 6 * NSA_KV_DIM].reshape(B, T, 2, NSA_KV_GROUPS, NSA_DH)
    gates = jax.nn.sigmoid(h[..., qd + 6 * NSA_KV_DIM:])
    return q, kv, win, gates


def compress_blocks(kv, pe, w1, w2):
    B, L = kv.shape[:2]
    n = (L - CMP_LEN) // CMP_STRIDE + 1
    idx = jnp.arange(n)[:, None] * CMP_STRIDE + jnp.arange(CMP_LEN)[None, :]
    blk = kv[:, idx] + pe[:, None, :]
    blk = jnp.moveaxis(blk, 3, 2).reshape(B, n, NSA_KV_GROUPS, CMP_LEN * NSA_DH)
    return jax.nn.silu(blk @ w1) @ w2


def selection_blocks(k):
    B, L = k.shape[:2]
    n_sel = -(-L // SEL_BLOCK)
    k = jnp.pad(k, ((0, 0), (0, n_sel * SEL_BLOCK - L), (0, 0), (0, 0)))
    return jnp.transpose(k.reshape(B, n_sel, SEL_BLOCK, NSA_KV_GROUPS, NSA_DH), (0, 3, 1, 2, 4))


def cmp_to_sel_overlap(n_cmp, n_sel):
    cs = jnp.arange(n_cmp)[:, None] * CMP_STRIDE
    ss = jnp.arange(n_sel)[None, :] * SEL_BLOCK
    return ((cs < ss + SEL_BLOCK) & (cs + CMP_LEN > ss)).astype(jnp.float32)


def branch_keys(kv_rows, pe_k, pe_v, w1k, w2k, w1v, w2v):
    kc = compress_blocks(kv_rows[:, :, 0], pe_k, w1k, w2k)
    vc = compress_blocks(kv_rows[:, :, 1], pe_v, w1v, w2v)
    k_sel = selection_blocks(kv_rows[:, :, 2])
    v_sel = selection_blocks(kv_rows[:, :, 3])
    overlap = cmp_to_sel_overlap(kc.shape[1], k_sel.shape[2])
    return kc, vc, k_sel, v_sel, overlap


def nsa_attend(q, q_pos, kc, vc, k_sel, v_sel, overlap, k_win, v_win, win_pos, gates):
    B, Tq = q.shape[:2]
    qg = q.reshape(B, Tq, NSA_KV_GROUPS, NSA_HPG, NSA_DH)
    cmp_end = jnp.arange(kc.shape[1]) * CMP_STRIDE + CMP_LEN - 1
    m_c = cmp_end[None, :] <= q_pos[:, None]
    s_c = jnp.einsum('bqghd,bngd->bghqn', qg, kc).astype(jnp.float32)
    p_c = masked_softmax(s_c, m_c)
    o_c = jnp.einsum('bghqn,bngd->bqghd', p_c.astype(vc.dtype), vc)
    n_sel = k_sel.shape[2]
    imp = jnp.einsum('bghqn,nj->bgqj', p_c, overlap)
    cur = q_pos // SEL_BLOCK
    j = jnp.arange(n_sel)[None, :]
    forced = (j == 0) | (j == cur[:, None]) | (j == cur[:, None] - 1)
    score = jnp.where(j > cur[:, None], -jnp.inf, jnp.where(forced, jnp.inf, imp))
    _, idx = lax.top_k(score, min(SEL_TOP, n_sel))
    n_top = idx.shape[-1]
    b_ix = jnp.arange(B)[:, None, None, None]
    g_ix = jnp.arange(NSA_KV_GROUPS)[None, :, None, None]
    ks = k_sel[b_ix, g_ix, idx].reshape(B, NSA_KV_GROUPS, Tq, n_top * SEL_BLOCK, NSA_DH)
    vs = v_sel[b_ix, g_ix, idx].reshape(B, NSA_KV_GROUPS, Tq, n_top * SEL_BLOCK, NSA_DH)
    pos = (idx[..., None] * SEL_BLOCK + jnp.arange(SEL_BLOCK)).reshape(B, NSA_KV_GROUPS, Tq, n_top * SEL_BLOCK)
    m_s = (pos <= q_pos[:, None])[:, :, None]
    s_s = jnp.einsum('bqghd,bgqkd->bghqk', qg, ks).astype(jnp.float32)
    p_s = masked_softmax(s_s, m_s)
    o_s = jnp.einsum('bghqk,bgqkd->bqghd', p_s.astype(vs.dtype), vs)
    m_w = (win_pos[None, :] <= q_pos[:, None]) & (win_pos[None, :] >= q_pos[:, None] - WINDOW) & (win_pos[None, :] >= 0)
    s_w = jnp.einsum('bqghd,bkgd->bghqk', qg, k_win).astype(jnp.float32)
    p_w = masked_softmax(s_w, m_w)
    o_w = jnp.einsum('bghqk,bkgd->bqghd', p_w.astype(v_win.dtype), v_win)
    g = gates.reshape(B, Tq, NSA_KV_GROUPS, NSA_HPG, 3)
    o = g[..., 0:1] * o_c + g[..., 1:2] * o_s + g[..., 2:3] * o_w
    return o.reshape(B, Tq, NSA_HEADS * NSA_DH)


def nsa_prompt(x, win_buf_len, w_in, pe_k, pe_v, w1k, w2k, w1v, w2v, w_out):
    B, T, _ = x.shape
    q, kv_rows, win_rows, gates = nsa_project(x, w_in)
    kc, vc, k_sel, v_sel, overlap = branch_keys(kv_rows, pe_k, pe_v, w1k, w2k, w1v, w2v)
    zpad = jnp.zeros((B, WINDOW, NSA_KV_GROUPS, NSA_DH), x.dtype)
    kw_pad = jnp.concatenate([zpad, win_rows[:, :, 0]], axis=1)
    vw_pad = jnp.concatenate([zpad, win_rows[:, :, 1]], axis=1)
    nb = T // Q_BLOCK
    qb = jnp.moveaxis(q.reshape(B, nb, Q_BLOCK, NSA_HEADS, NSA_DH), 1, 0)
    gb = jnp.moveaxis(gates.reshape(B, nb, Q_BLOCK, 3 * NSA_HEADS), 1, 0)

    def block(inp):
        qi, gi, i = inp
        start = i * Q_BLOCK
        q_pos = start + jnp.arange(Q_BLOCK)
        kw = lax.dynamic_slice_in_dim(kw_pad, start, WINDOW + Q_BLOCK, axis=1)
        vw = lax.dynamic_slice_in_dim(vw_pad, start, WINDOW + Q_BLOCK, axis=1)
        win_pos = start - WINDOW + jnp.arange(WINDOW + Q_BLOCK)
        return nsa_attend(qi, q_pos, kc, vc, k_sel, v_sel, overlap, kw, vw, win_pos, gi)

    o = lax.map(block, (qb, gb, jnp.arange(nb)))
    o = jnp.moveaxis(o, 0, 1).reshape(B, T, NSA_HEADS * NSA_DH) @ w_out
    wbuf = jnp.concatenate([jnp.zeros((B, win_buf_len, 2, NSA_KV_GROUPS, NSA_DH), x.dtype), win_rows], axis=1)[:, T:]
    return o, kv_rows, wbuf


def nsa_sample(x, kv_pool, win_cache, page_table, w_in, pe_k, pe_v, w1k, w2k, w1v, w2v, w_out):
    B, T, _ = x.shape
    q, kv_new, win_new, gates = nsa_project(x, w_in)
    past = kv_pool[page_table]
    past = past.reshape((B, -1) + past.shape[3:])
    past_len = past.shape[1]
    kv_all = jnp.concatenate([past, kv_new], axis=1)
    kc, vc, k_sel, v_sel, overlap = branch_keys(kv_all, pe_k, pe_v, w1k, w2k, w1v, w2v)
    win_all = jnp.concatenate([win_cache, win_new], axis=1)
    win_buf_len = win_cache.shape[1]
    win_pos = past_len - win_buf_len + jnp.arange(win_buf_len + T)
    q_pos = past_len + jnp.arange(T)
    o = nsa_attend(q, q_pos, kc, vc, k_sel, v_sel, overlap, win_all[:, :, 0], win_all[:, :, 1], win_pos, gates)
    return o @ w_out, kv_new, win_all[:, T:]


def setup_inputs(seed: int = 0) -> dict:
    key = jax.random.key(seed)
    ks = jax.random.split(key, 32)
    f32 = jnp.float32
    n_pages = PAST_LEN // PAGE_SIZE
    n_phys = (DEC_BATCH * n_pages * 5) // 4
    win_buf = min(WINDOW, PAST_LEN)

    def nrm(k, shape, scale):
        return jax.random.normal(k, shape, f32) * scale

    dt = jnp.exp(jax.random.uniform(ks[10], (N_GDN, GDN_HEADS), f32, np.log(1e-3), np.log(1e-1)))
    page_table = jax.random.permutation(ks[6], n_phys)[:DEC_BATCH * n_pages].reshape(DEC_BATCH, n_pages).astype(jnp.int32)
    return {
        'x_prompt': nrm(ks[0], (BATCH, SEQ, D_MODEL), 1.0),
        'x_sample': nrm(ks[1], (DEC_BATCH, DEC_SEQ, D_MODEL), 1.0),
        'cache_nsa_kv': nrm(ks[2], (N_NSA, n_phys, PAGE_SIZE, 4, NSA_KV_GROUPS, NSA_DH), 1.0),
        'cache_nsa_win': nrm(ks[3], (N_NSA, DEC_BATCH, win_buf, 2, NSA_KV_GROUPS, NSA_DH), 1.0),
        'state_gdn': nrm(ks[4], (N_GDN, DEC_BATCH, GDN_HEADS, GDN_DK, GDN_DV), 0.1),
        'state_gdn_conv': nrm(ks[5], (N_GDN, DEC_BATCH, GDN_CONV - 1, GDN_QKV), 1.0),
        'page_table': page_table,
        'norm1_w': 1.0 + nrm(ks[7], (DEPTH, D_MODEL), 0.02),
        'norm2_w': 1.0 + nrm(ks[8], (DEPTH, D_MODEL), 0.02),
        'final_norm_w': 1.0 + nrm(ks[9], (D_MODEL,), 0.02),
        'gdn_w_in': nrm(ks[11], (N_GDN, D_MODEL, GDN_IN), D_MODEL ** -0.5),
        'gdn_conv_w': nrm(ks[12], (N_GDN, GDN_CONV, GDN_QKV), GDN_CONV ** -0.5),
        'gdn_A_log': jnp.log(jax.random.uniform(ks[13], (N_GDN, GDN_HEADS), f32, 1.0, 16.0)),
        'gdn_dt_bias': dt + jnp.log(-jnp.expm1(-dt)),
        'gdn_norm_w': 1.0 + nrm(ks[14], (N_GDN, GDN_DV), 0.02),
        'gdn_w_out': nrm(ks[15], (N_GDN, GDN_HEADS * GDN_DV, D_MODEL), (GDN_HEADS * GDN_DV) ** -0.5),
        'nsa_w_in': nrm(ks[16], (N_NSA, D_MODEL, NSA_IN), D_MODEL ** -0.5),
        'nsa_pe_k': nrm(ks[17], (N_NSA, CMP_LEN, NSA_DH), 0.1),
        'nsa_pe_v': nrm(ks[18], (N_NSA, CMP_LEN, NSA_DH), 0.1),
        'nsa_cmp_w1_k': nrm(ks[19], (N_NSA, CMP_LEN * NSA_DH, CMP_HID), (CMP_LEN * NSA_DH) ** -0.5),
        'nsa_cmp_w2_k': nrm(ks[20], (N_NSA, CMP_HID, NSA_DH), CMP_HID ** -0.5),
        'nsa_cmp_w1_v': nrm(ks[21], (N_NSA, CMP_LEN * NSA_DH, CMP_HID), (CMP_LEN * NSA_DH) ** -0.5),
        'nsa_cmp_w2_v': nrm(ks[22], (N_NSA, CMP_HID, NSA_DH), CMP_HID ** -0.5),
        'nsa_w_out': nrm(ks[23], (N_NSA, NSA_HEADS * NSA_DH, D_MODEL), (NSA_HEADS * NSA_DH) ** -0.5),
        'ffn_w_in': nrm(ks[24], (DEPTH, D_MODEL, 2 * D_FF), D_MODEL ** -0.5),
        'ffn_w_out': nrm(ks[25], (DEPTH, D_FF, D_MODEL), D_FF ** -0.5),
    }


def reference(x_prompt, x_sample, cache_nsa_kv, cache_nsa_win, state_gdn, state_gdn_conv, page_table,
              norm1_w, norm2_w, final_norm_w, gdn_w_in, gdn_conv_w, gdn_A_log, gdn_dt_bias, gdn_norm_w, gdn_w_out,
              nsa_w_in, nsa_pe_k, nsa_pe_v, nsa_cmp_w1_k, nsa_cmp_w2_k, nsa_cmp_w1_v, nsa_cmp_w2_v, nsa_w_out,
              ffn_w_in, ffn_w_out):
    xp, xs = x_prompt, x_sample
    bp, tp = xp.shape[:2]
    win_buf_len = cache_nsa_win.shape[2]
    kv_p, kv_s, win_p, win_s = [], [], [], []
    gs_p, gs_s, gc_p, gc_s = [], [], [], []
    for i in range(DEPTH):
        li = i // N_MIXERS
        hp = rmsnorm(xp, norm1_w[i])
        hs = rmsnorm(xs, norm1_w[i])
        if i % N_MIXERS == 0:
            gw = (gdn_w_in[li], gdn_conv_w[li], gdn_A_log[li], gdn_dt_bias[li], gdn_norm_w[li], gdn_w_out[li])
            s0 = jnp.zeros((bp, GDN_HEADS, GDN_DK, GDN_DV), xp.dtype)
            c0 = jnp.zeros((bp, GDN_CONV - 1, GDN_QKV), xp.dtype)
            mp, sp, cp = gdn_mixer(hp, s0, c0, min(GDN_CHUNK, tp), *gw)
            ms, ss, cs = gdn_mixer(hs, state_gdn[li], state_gdn_conv[li], xs.shape[1], *gw)
            gs_p.append(sp)
            gs_s.append(ss)
            gc_p.append(cp)
            gc_s.append(cs)
        else:
            nw = (nsa_w_in[li], nsa_pe_k[li], nsa_pe_v[li], nsa_cmp_w1_k[li], nsa_cmp_w2_k[li],
                  nsa_cmp_w1_v[li], nsa_cmp_w2_v[li], nsa_w_out[li])
            mp, kp, wp = nsa_prompt(hp, win_buf_len, *nw)
            ms, ksm, wsm = nsa_sample(hs, cache_nsa_kv[li], cache_nsa_win[li], page_table, *nw)
            kv_p.append(kp)
            kv_s.append(ksm)
            win_p.append(wp)
            win_s.append(wsm)
        xp = xp + mp
        xs = xs + ms
        xp = xp + swiglu_ffn(rmsnorm(xp, norm2_w[i]), ffn_w_in[i], ffn_w_out[i])
        xs = xs + swiglu_ffn(rmsnorm(xs, norm2_w[i]), ffn_w_in[i], ffn_w_out[i])
    y_prompt = rmsnorm(xp, final_norm_w)
    y_sample = rmsnorm(xs, final_norm_w)
    return (y_prompt, y_sample, jnp.stack(kv_p), jnp.stack(kv_s), jnp.stack(win_p), jnp.stack(win_s),
            jnp.stack(gs_p), jnp.stack(gs_s), jnp.stack(gc_p), jnp.stack(gc_s))
```

```python
import functools

import jax
import jax.numpy as jnp
from jax import lax
from jax.experimental import pallas as pl
from jax.experimental.pallas import tpu as pltpu

D_MODEL = 1024
DEPTH = 4
PAGE_SIZE = 128
N_MIXERS = 2
GDN_HEADS = 8
GDN_DK = 128
GDN_DV = 128
GDN_CONV = 4
GDN_CHUNK = 64
GDN_QKV = GDN_HEADS * (2 * GDN_DK + GDN_DV)
GDN_IN = GDN_QKV + GDN_HEADS * GDN_DV + 2 * GDN_HEADS
NSA_HEADS = 16
NSA_KV_GROUPS = 2
NSA_HPG = NSA_HEADS // NSA_KV_GROUPS
NSA_DH = 64
NSA_KV_DIM = NSA_KV_GROUPS * NSA_DH
CMP_LEN = 32
CMP_STRIDE = 16
CMP_HID = 256
SEL_BLOCK = 64
SEL_TOP = 16
WINDOW = 512
Q_BLOCK = 128
NSA_IN = NSA_HEADS * NSA_DH + 6 * NSA_KV_DIM + 3 * NSA_HEADS
D_FF = ((-(-8 * D_MODEL // 3) + 255) // 256) * 256
RMS_EPS = 1e-6
L2_EPS = 1e-6
NEG_BIG = -1e30

LANES = 128
VMEM_LIMIT = 56 * 1024 * 1024
ROW_TILE = 512
FF_CHUNK = 256

BF16 = jnp.bfloat16
F32 = jnp.float32


def _round_up(n, m):
    return -(-n // m) * m


def _rms_rows(x, w):
    return x * lax.rsqrt(jnp.mean(x * x, axis=-1, keepdims=True) + RMS_EPS) * w


def _compiler_params(n_axes):
    return pltpu.CompilerParams(dimension_semantics=("arbitrary",) * n_axes, vmem_limit_bytes=VMEM_LIMIT)


def _resident(shape):
    return pl.BlockSpec(shape, lambda *_: (0,) * len(shape), pipeline_mode=pl.Buffered(1))


def _norm_proj_kernel(x_ref, nw_ref, w_ref, o_ref):
    xn = _rms_rows(x_ref[...], nw_ref[...]).astype(BF16)
    n = w_ref.shape[1]
    for c0 in range(0, n, 512):
        c1 = min(c0 + 512, n)
        o_ref[:, c0:c1] = jnp.dot(xn, w_ref[:, c0:c1], preferred_element_type=F32)


def norm_proj(x, nw, w):
    m, d = x.shape
    n = w.shape[1]
    tm = 256
    return pl.pallas_call(
        _norm_proj_kernel,
        out_shape=jax.ShapeDtypeStruct((m, n), F32),
        grid=(m // tm,),
        in_specs=[pl.BlockSpec((tm, d), lambda i: (i, 0)), _resident((1, d)), _resident((d, n))],
        out_specs=pl.BlockSpec((tm, n), lambda i: (i, 0)),
        compiler_params=_compiler_params(1),
        name="norm_proj",
    )(x, nw.reshape(1, d), w)


def _proj_residual_kernel(x_ref, a_ref, w_ref, o_ref):
    o_ref[...] = x_ref[...] + jnp.dot(a_ref[...].astype(BF16), w_ref[...], preferred_element_type=F32)


def proj_residual(x, a, w):
    m, d = x.shape
    k = a.shape[1]
    tm = ROW_TILE
    return pl.pallas_call(
        _proj_residual_kernel,
        out_shape=jax.ShapeDtypeStruct((m, d), F32),
        grid=(m // tm,),
        in_specs=[pl.BlockSpec((tm, d), lambda i: (i, 0)), pl.BlockSpec((tm, k), lambda i: (i, 0)), _resident((k, d))],
        out_specs=pl.BlockSpec((tm, d), lambda i: (i, 0)),
        compiler_params=_compiler_params(1),
        name="proj_residual",
    )(x, a, w)


def _ffn_kernel(x_ref, nw_ref, wi_ref, wo_ref, fw_ref, o_ref, act_ref, *, final_norm):
    x = x_ref[...]
    xn = _rms_rows(x, nw_ref[...]).astype(BF16)
    for c0 in range(0, D_FF, FF_CHUNK):
        g = jnp.dot(xn, wi_ref[:, c0:c0 + FF_CHUNK], preferred_element_type=F32)
        u = jnp.dot(xn, wi_ref[:, D_FF + c0:D_FF + c0 + FF_CHUNK], preferred_element_type=F32)
        act_ref[:, c0:c0 + FF_CHUNK] = (g * jax.nn.sigmoid(g) * u).astype(BF16)
    y = x + jnp.dot(act_ref[...], wo_ref[...], preferred_element_type=F32)
    if final_norm:
        y = _rms_rows(y, fw_ref[...])
    o_ref[...] = y


def ffn(x, nw, w_in, w_out, final_w, final_norm):
    m, d = x.shape
    tm = ROW_TILE
    return pl.pallas_call(
        functools.partial(_ffn_kernel, final_norm=final_norm),
        out_shape=jax.ShapeDtypeStruct((m, d), F32),
        grid=(m // tm,),
        in_specs=[pl.BlockSpec((tm, d), lambda i: (i, 0)), _resident((1, d)), _resident((d, 2 * D_FF)),
                  _resident((D_FF, d)), _resident((1, d))],
        out_specs=pl.BlockSpec((tm, d), lambda i: (i, 0)),
        scratch_shapes=[pltpu.VMEM((tm, D_FF), BF16)],
        compiler_params=_compiler_params(1),
        name="ffn",
    )(x, nw.reshape(1, d), w_in, w_out, final_w.reshape(1, d))


def rmsnorm(x, w):
    xf = x.astype(jnp.float32)
    y = xf * lax.rsqrt(jnp.mean(xf * xf, axis=-1, keepdims=True) + RMS_EPS)
    return (y * w.astype(jnp.float32)).astype(x.dtype)


def l2norm(x):
    xf = x.astype(jnp.float32)
    return xf * lax.rsqrt(jnp.sum(xf * xf, axis=-1, keepdims=True) + L2_EPS)


def masked_softmax(s, mask):
    p = jax.nn.softmax(jnp.where(mask, s, NEG_BIG), axis=-1)
    return jnp.where(mask, p, 0.0)


def causal_short_conv(x, buf, w):
    xp = jnp.concatenate([buf, x], axis=1)
    T = x.shape[1]
    y = xp[:, 0:T] * w[0]
    for i in range(1, GDN_CONV):
        y = y + xp[:, i:i + T] * w[i]
    return jax.nn.silu(y), xp[:, T:]


def gated_delta_chunked(q, k, v, log_a, beta, S0, chunk):
    B, T, H, dk = q.shape
    dv = v.shape[-1]
    n = T // chunk

    def to_chunks(t):
        t = t.reshape((B, n, chunk, H) + t.shape[3:])
        return jnp.moveaxis(t, (1, 3), (0, 2))

    tri = jnp.tril(jnp.ones((chunk, chunk), bool))
    strict = jnp.tril(jnp.ones((chunk, chunk), bool), -1)
    eye = jnp.eye(chunk, dtype=jnp.float32)

    def step(S, inp):
        qc, kc, vc, gc, bc = inp
        G = jnp.cumsum(gc, axis=-1)
        decay = jnp.exp(jnp.where(tri, G[..., :, None] - G[..., None, :], -jnp.inf))
        lmat = jnp.where(strict, bc[..., :, None] * jnp.einsum('bhik,bhjk->bhij', kc, kc) * decay, 0.0)
        eg = jnp.exp(G)
        rhs = bc[..., None] * (vc - eg[..., None] * jnp.einsum('bhck,bhkv->bhcv', kc, S))
        u = lax.linalg.triangular_solve(eye + lmat, rhs, left_side=True, lower=True, unit_diagonal=True)
        o = eg[..., None] * jnp.einsum('bhck,bhkv->bhcv', qc, S) + jnp.einsum(
            'bhij,bhjv->bhiv', jnp.einsum('bhik,bhjk->bhij', qc, kc) * decay, u)
        tail = jnp.exp(G[..., -1:] - G)
        S_new = jnp.exp(G[..., -1])[..., None, None] * S + jnp.einsum('bhck,bhcv->bhkv', kc * tail[..., None], u)
        return S_new, o

    S, o = lax.scan(step, S0, (to_chunks(q), to_chunks(k), to_chunks(v), to_chunks(log_a), to_chunks(beta)))
    o = jnp.moveaxis(o, (0, 2), (1, 3)).reshape(B, T, H, dv)
    return o, S


def gdn_core(h, S0, conv_buf, chunk, conv_w, A_log, dt_bias, norm_w):
    B, T, _ = h.shape
    z_end = GDN_QKV + GDN_HEADS * GDN_DV
    qkv, z, b, a = jnp.split(h[..., :GDN_IN], [GDN_QKV, z_end, z_end + GDN_HEADS], axis=-1)
    qkv, new_buf = causal_short_conv(qkv, conv_buf, conv_w)
    q, k, v = jnp.split(qkv, [GDN_HEADS * GDN_DK, 2 * GDN_HEADS * GDN_DK], axis=-1)
    q = l2norm(q.reshape(B, T, GDN_HEADS, GDN_DK)) * GDN_DK ** -0.5
    k = l2norm(k.reshape(B, T, GDN_HEADS, GDN_DK))
    v = v.reshape(B, T, GDN_HEADS, GDN_DV).astype(jnp.float32)
    beta = jax.nn.sigmoid(b.astype(jnp.float32))
    log_a = -jnp.exp(A_log.astype(jnp.float32)) * jax.nn.softplus(a.astype(jnp.float32) + dt_bias.astype(jnp.float32))
    o, S = gated_delta_chunked(q, k, v, log_a, beta, S0.astype(jnp.float32), chunk)
    o = rmsnorm(o, norm_w) * jax.nn.silu(z.reshape(B, T, GDN_HEADS, GDN_DV).astype(jnp.float32))
    return o.reshape(B, T, GDN_HEADS * GDN_DV), S, new_buf


def nsa_split(h):
    B, T, _ = h.shape
    qd = NSA_HEADS * NSA_DH
    q = h[..., :qd].reshape(B, T, NSA_HEADS, NSA_DH) * NSA_DH ** -0.5
    kv = h[..., qd:qd + 4 * NSA_KV_DIM].reshape(B, T, 4, NSA_KV_GROUPS, NSA_DH)
    win = h[..., qd + 4 * NSA_KV_DIM:qd + 6 * NSA_KV_DIM].reshape(B, T, 2, NSA_KV_GROUPS, NSA_DH)
    gates = jax.nn.sigmoid(h[..., qd + 6 * NSA_KV_DIM:NSA_IN])
    return q, kv, win, gates


def compress_blocks(kv, pe, w1, w2):
    B, L = kv.shape[:2]
    n = (L - CMP_LEN) // CMP_STRIDE + 1
    idx = jnp.arange(n)[:, None] * CMP_STRIDE + jnp.arange(CMP_LEN)[None, :]
    blk = kv[:, idx] + pe[:, None, :]
    blk = jnp.moveaxis(blk, 3, 2).reshape(B, n, NSA_KV_GROUPS, CMP_LEN * NSA_DH)
    return jax.nn.silu(blk @ w1) @ w2


def selection_blocks(k):
    B, L = k.shape[:2]
    n_sel = -(-L // SEL_BLOCK)
    k = jnp.pad(k, ((0, 0), (0, n_sel * SEL_BLOCK - L), (0, 0), (0, 0)))
    return jnp.transpose(k.reshape(B, n_sel, SEL_BLOCK, NSA_KV_GROUPS, NSA_DH), (0, 3, 1, 2, 4))


def cmp_to_sel_overlap(n_cmp, n_sel):
    cs = jnp.arange(n_cmp)[:, None] * CMP_STRIDE
    ss = jnp.arange(n_sel)[None, :] * SEL_BLOCK
    return ((cs < ss + SEL_BLOCK) & (cs + CMP_LEN > ss)).astype(jnp.float32)


def branch_keys(kv_rows, pe_k, pe_v, w1k, w2k, w1v, w2v):
    kc = compress_blocks(kv_rows[:, :, 0], pe_k, w1k, w2k)
    vc = compress_blocks(kv_rows[:, :, 1], pe_v, w1v, w2v)
    k_sel = selection_blocks(kv_rows[:, :, 2])
    v_sel = selection_blocks(kv_rows[:, :, 3])
    overlap = cmp_to_sel_overlap(kc.shape[1], k_sel.shape[2])
    return kc, vc, k_sel, v_sel, overlap


def nsa_attend(q, q_pos, kc, vc, k_sel, v_sel, overlap, k_win, v_win, win_pos, gates):
    B, Tq = q.shape[:2]
    qg = q.reshape(B, Tq, NSA_KV_GROUPS, NSA_HPG, NSA_DH)
    cmp_end = jnp.arange(kc.shape[1]) * CMP_STRIDE + CMP_LEN - 1
    m_c = cmp_end[None, :] <= q_pos[:, None]
    s_c = jnp.einsum('bqghd,bngd->bghqn', qg, kc).astype(jnp.float32)
    p_c = masked_softmax(s_c, m_c)
    o_c = jnp.einsum('bghqn,bngd->bqghd', p_c.astype(vc.dtype), vc)
    n_sel = k_sel.shape[2]
    imp = jnp.einsum('bghqn,nj->bgqj', p_c, overlap)
    cur = q_pos // SEL_BLOCK
    j = jnp.arange(n_sel)[None, :]
    forced = (j == 0) | (j == cur[:, None]) | (j == cur[:, None] - 1)
    score = jnp.where(j > cur[:, None], -jnp.inf, jnp.where(forced, jnp.inf, imp))
    _, idx = lax.top_k(score, min(SEL_TOP, n_sel))
    n_top = idx.shape[-1]
    b_ix = jnp.arange(B)[:, None, None, None]
    g_ix = jnp.arange(NSA_KV_GROUPS)[None, :, None, None]
    ks = k_sel[b_ix, g_ix, idx].reshape(B, NSA_KV_GROUPS, Tq, n_top * SEL_BLOCK, NSA_DH)
    vs = v_sel[b_ix, g_ix, idx].reshape(B, NSA_KV_GROUPS, Tq, n_top * SEL_BLOCK, NSA_DH)
    pos = (idx[..., None] * SEL_BLOCK + jnp.arange(SEL_BLOCK)).reshape(B, NSA_KV_GROUPS, Tq, n_top * SEL_BLOCK)
    m_s = (pos <= q_pos[:, None])[:, :, None]
    s_s = jnp.einsum('bqghd,bgqkd->bghqk', qg, ks).astype(jnp.float32)
    p_s = masked_softmax(s_s, m_s)
    o_s = jnp.einsum('bghqk,bgqkd->bqghd', p_s.astype(vs.dtype), vs)
    m_w = (win_pos[None, :] <= q_pos[:, None]) & (win_pos[None, :] >= q_pos[:, None] - WINDOW) & (win_pos[None, :] >= 0)
    s_w = jnp.einsum('bqghd,bkgd->bghqk', qg, k_win).astype(jnp.float32)
    p_w = masked_softmax(s_w, m_w)
    o_w = jnp.einsum('bghqk,bkgd->bqghd', p_w.astype(v_win.dtype), v_win)
    g = gates.reshape(B, Tq, NSA_KV_GROUPS, NSA_HPG, 3)
    o = g[..., 0:1] * o_c + g[..., 1:2] * o_s + g[..., 2:3] * o_w
    return o.reshape(B, Tq, NSA_HEADS * NSA_DH)


def nsa_prompt_core(h, win_buf_len, pe_k, pe_v, w1k, w2k, w1v, w2v):
    B, T, _ = h.shape
    q, kv_rows, win_rows, gates = nsa_split(h)
    kc, vc, k_sel, v_sel, overlap = branch_keys(kv_rows, pe_k, pe_v, w1k, w2k, w1v, w2v)
    zpad = jnp.zeros((B, WINDOW, NSA_KV_GROUPS, NSA_DH), h.dtype)
    kw_pad = jnp.concatenate([zpad, win_rows[:, :, 0]], axis=1)
    vw_pad = jnp.concatenate([zpad, win_rows[:, :, 1]], axis=1)
    nb = T // Q_BLOCK
    qb = jnp.moveaxis(q.reshape(B, nb, Q_BLOCK, NSA_HEADS, NSA_DH), 1, 0)
    gb = jnp.moveaxis(gates.reshape(B, nb, Q_BLOCK, 3 * NSA_HEADS), 1, 0)

    def block(inp):
        qi, gi, i = inp
        start = i * Q_BLOCK
        q_pos = start + jnp.arange(Q_BLOCK)
        kw = lax.dynamic_slice_in_dim(kw_pad, start, WINDOW + Q_BLOCK, axis=1)
        vw = lax.dynamic_slice_in_dim(vw_pad, start, WINDOW + Q_BLOCK, axis=1)
        win_pos = start - WINDOW + jnp.arange(WINDOW + Q_BLOCK)
        return nsa_attend(qi, q_pos, kc, vc, k_sel, v_sel, overlap, kw, vw, win_pos, gi)

    o = lax.map(block, (qb, gb, jnp.arange(nb)))
    o = jnp.moveaxis(o, 0, 1).reshape(B, T, NSA_HEADS * NSA_DH)
    wbuf = jnp.concatenate([jnp.zeros((B, win_buf_len, 2, NSA_KV_GROUPS, NSA_DH), h.dtype), win_rows], axis=1)[:, T:]
    return o, kv_rows, wbuf


def nsa_sample_core(h, kv_pool, win_cache, page_table, pe_k, pe_v, w1k, w2k, w1v, w2v):
    B, T, _ = h.shape
    q, kv_new, win_new, gates = nsa_split(h)
    past = kv_pool[page_table]
    past = past.reshape((B, -1) + past.shape[3:])
    past_len = past.shape[1]
    kv_all = jnp.concatenate([past, kv_new], axis=1)
    kc, vc, k_sel, v_sel, overlap = branch_keys(kv_all, pe_k, pe_v, w1k, w2k, w1v, w2v)
    win_all = jnp.concatenate([win_cache, win_new], axis=1)
    win_buf_len = win_cache.shape[1]
    win_pos = past_len - win_buf_len + jnp.arange(win_buf_len + T)
    q_pos = past_len + jnp.arange(T)
    o = nsa_attend(q, q_pos, kc, vc, k_sel, v_sel, overlap, win_all[:, :, 0], win_all[:, :, 1], win_pos, gates)
    return o, kv_new, win_all[:, T:]


def _pad_cols(w, n):
    return jnp.pad(w, ((0, 0), (0, n - w.shape[1])))


def kernel(x_prompt, x_sample, cache_nsa_kv, cache_nsa_win, state_gdn, state_gdn_conv, page_table,
           norm1_w, norm2_w, final_norm_w, gdn_w_in, gdn_conv_w, gdn_A_log, gdn_dt_bias, gdn_norm_w, gdn_w_out,
           nsa_w_in, nsa_pe_k, nsa_pe_v, nsa_cmp_w1_k, nsa_cmp_w2_k, nsa_cmp_w1_v, nsa_cmp_w2_v, nsa_w_out,
           ffn_w_in, ffn_w_out):
    bp, tp, d = x_prompt.shape
    bs, ts, _ = x_sample.shape
    mp, ms = bp * tp, bs * ts
    win_buf_len = cache_nsa_win.shape[2]
    x = jnp.concatenate([x_prompt.reshape(mp, d), x_sample.reshape(ms, d)], axis=0)
    gdn_n = _round_up(GDN_IN, LANES)
    nsa_n = _round_up(NSA_IN, LANES)
    kv_p, kv_s, win_p, win_s = [], [], [], []
    gs_p, gs_s, gc_p, gc_s = [], [], [], []
    for i in range(DEPTH):
        li = i // N_MIXERS
        if i % N_MIXERS == 0:
            h = norm_proj(x, norm1_w[i], _pad_cols(gdn_w_in[li], gdn_n).astype(BF16))
            hp = h[:mp].reshape(bp, tp, gdn_n)
            hs = h[mp:].reshape(bs, ts, gdn_n)
            gw = (gdn_conv_w[li], gdn_A_log[li], gdn_dt_bias[li], gdn_norm_w[li])
            s0 = jnp.zeros((bp, GDN_HEADS, GDN_DK, GDN_DV), F32)
            c0 = jnp.zeros((bp, GDN_CONV - 1, GDN_QKV), F32)
            op, sp, cp = gdn_core(hp, s0, c0, min(GDN_CHUNK, tp), *gw)
            os_, ss, cs = gdn_core(hs, state_gdn[li], state_gdn_conv[li], ts, *gw)
            gs_p.append(sp)
            gs_s.append(ss)
            gc_p.append(cp)
            gc_s.append(cs)
            o = jnp.concatenate([op.reshape(mp, -1), os_.reshape(ms, -1)], axis=0)
            x = proj_residual(x, o, gdn_w_out[li].astype(BF16))
        else:
            h = norm_proj(x, norm1_w[i], _pad_cols(nsa_w_in[li], nsa_n).astype(BF16))
            hp = h[:mp].reshape(bp, tp, nsa_n)
            hs = h[mp:].reshape(bs, ts, nsa_n)
            nw = (nsa_pe_k[li], nsa_pe_v[li], nsa_cmp_w1_k[li], nsa_cmp_w2_k[li], nsa_cmp_w1_v[li], nsa_cmp_w2_v[li])
            op, kp, wp = nsa_prompt_core(hp, win_buf_len, *nw)
            os_, ksm, wsm = nsa_sample_core(hs, cache_nsa_kv[li], cache_nsa_win[li], page_table, *nw)
            kv_p.append(kp)
            kv_s.append(ksm)
            win_p.append(wp)
            win_s.append(wsm)
            o = jnp.concatenate([op.reshape(mp, -1), os_.reshape(ms, -1)], axis=0)
            x = proj_residual(x, o, nsa_w_out[li].astype(BF16))
        x = ffn(x, norm2_w[i], ffn_w_in[i].astype(BF16), ffn_w_out[i].astype(BF16), final_norm_w, i == DEPTH - 1)
    y_prompt = x[:mp].reshape(bp, tp, d)
    y_sample = x[mp:].reshape(bs, ts, d)
    return (y_prompt, y_sample, jnp.stack(kv_p), jnp.stack(kv_s), jnp.stack(win_p), jnp.stack(win_s),
            jnp.stack(gs_p), jnp.stack(gs_s), jnp.stack(gc_p), jnp.stack(gc_s))
```

```python
import functools

import jax
import jax.numpy as jnp
from jax import lax
from jax.experimental import pallas as pl
from jax.experimental.pallas import tpu as pltpu

D_MODEL = 1024
DEPTH = 4
PAGE_SIZE = 128
N_MIXERS = 2
GDN_HEADS = 8
GDN_DK = 128
GDN_DV = 128
GDN_CONV = 4
GDN_CHUNK = 64
GDN_QKV = GDN_HEADS * (2 * GDN_DK + GDN_DV)
GDN_IN = GDN_QKV + GDN_HEADS * GDN_DV + 2 * GDN_HEADS
NSA_HEADS = 16
NSA_KV_GROUPS = 2
NSA_HPG = NSA_HEADS // NSA_KV_GROUPS
NSA_DH = 64
NSA_KV_DIM = NSA_KV_GROUPS * NSA_DH
CMP_LEN = 32
CMP_STRIDE = 16
CMP_HID = 256
SEL_BLOCK = 64
SEL_TOP = 16
WINDOW = 512
Q_BLOCK = 128
NSA_IN = NSA_HEADS * NSA_DH + 6 * NSA_KV_DIM + 3 * NSA_HEADS
D_FF = ((-(-8 * D_MODEL // 3) + 255) // 256) * 256
RMS_EPS = 1e-6
L2_EPS = 1e-6
NEG_BIG = -1e30

LANES = 128
VMEM_LIMIT = 56 * 1024 * 1024
ROW_TILE = 512
FF_CHUNK = 256

BF16 = jnp.bfloat16
F32 = jnp.float32


def _round_up(n, m):
    return -(-n // m) * m


def _rms_rows(x, w):
    return x * lax.rsqrt(jnp.mean(x * x, axis=-1, keepdims=True) + RMS_EPS) * w


def _compiler_params(n_axes):
    return pltpu.CompilerParams(dimension_semantics=("arbitrary",) * n_axes, vmem_limit_bytes=VMEM_LIMIT)


def _resident(shape):
    return pl.BlockSpec(shape, lambda *_: (0,) * len(shape), pipeline_mode=pl.Buffered(1))


def _norm_proj_kernel(x_ref, nw_ref, w_ref, o_ref):
    xn = _rms_rows(x_ref[...], nw_ref[...]).astype(BF16)
    n = w_ref.shape[1]
    for c0 in range(0, n, 512):
        c1 = min(c0 + 512, n)
        o_ref[:, c0:c1] = jnp.dot(xn, w_ref[:, c0:c1], preferred_element_type=F32)


def norm_proj(x, nw, w):
    m, d = x.shape
    n = w.shape[1]
    tm = 256
    return pl.pallas_call(
        _norm_proj_kernel,
        out_shape=jax.ShapeDtypeStruct((m, n), F32),
        grid=(m // tm,),
        in_specs=[pl.BlockSpec((tm, d), lambda i: (i, 0)), _resident((1, d)), _resident((d, n))],
        out_specs=pl.BlockSpec((tm, n), lambda i: (i, 0)),
        compiler_params=_compiler_params(1),
        name="norm_proj",
    )(x, nw.reshape(1, d), w)


def _proj_residual_kernel(x_ref, a_ref, w_ref, o_ref):
    o_ref[...] = x_ref[...] + jnp.dot(a_ref[...].astype(BF16), w_ref[...], preferred_element_type=F32)


def proj_residual(x, a, w):
    m, d = x.shape
    k = a.shape[1]
    tm = ROW_TILE
    return pl.pallas_call(
        _proj_residual_kernel,
        out_shape=jax.ShapeDtypeStruct((m, d), F32),
        grid=(m // tm,),
        in_specs=[pl.BlockSpec((tm, d), lambda i: (i, 0)), pl.BlockSpec((tm, k), lambda i: (i, 0)), _resident((k, d))],
        out_specs=pl.BlockSpec((tm, d), lambda i: (i, 0)),
        compiler_params=_compiler_params(1),
        name="proj_residual",
    )(x, a, w)


def _ffn_kernel(x_ref, nw_ref, wi_ref, wo_ref, fw_ref, o_ref, act_ref, *, final_norm):
    x = x_ref[...]
    xn = _rms_rows(x, nw_ref[...]).astype(BF16)
    for c0 in range(0, D_FF, FF_CHUNK):
        g = jnp.dot(xn, wi_ref[:, c0:c0 + FF_CHUNK], preferred_element_type=F32)
        u = jnp.dot(xn, wi_ref[:, D_FF + c0:D_FF + c0 + FF_CHUNK], preferred_element_type=F32)
        act_ref[:, c0:c0 + FF_CHUNK] = (g * jax.nn.sigmoid(g) * u).astype(BF16)
    y = x + jnp.dot(act_ref[...], wo_ref[...], preferred_element_type=F32)
    if final_norm:
        y = _rms_rows(y, fw_ref[...])
    o_ref[...] = y


def ffn(x, nw, w_in, w_out, final_w, final_norm):
    m, d = x.shape
    tm = ROW_TILE
    return pl.pallas_call(
        functools.partial(_ffn_kernel, final_norm=final_norm),
        out_shape=jax.ShapeDtypeStruct((m, d), F32),
        grid=(m // tm,),
        in_specs=[pl.BlockSpec((tm, d), lambda i: (i, 0)), _resident((1, d)), _resident((d, 2 * D_FF)),
                  _resident((D_FF, d)), _resident((1, d))],
        out_specs=pl.BlockSpec((tm, d), lambda i: (i, 0)),
        scratch_shapes=[pltpu.VMEM((tm, D_FF), BF16)],
        compiler_params=_compiler_params(1),
        name="ffn",
    )(x, nw.reshape(1, d), w_in, w_out, final_w.reshape(1, d))


GDN_N = _round_up(GDN_IN, LANES)
CONV_HALO = 8


def _gdn_prep_kernel(h_ref, halo_ref, cbuf_ref, cw_ref, alog_ref, dt_ref, qkv_ref, bg_ref, xbuf_ref, *, tt):
    t = pl.program_id(1)
    prev = jnp.where(t == 0, cbuf_ref[0], halo_ref[0])
    xbuf_ref[0:CONV_HALO, :] = prev
    xbuf_ref[CONV_HALO:CONV_HALO + tt, :] = h_ref[0, :, 0:GDN_QKV]
    for c in range(GDN_QKV // LANES):
        cs = slice(c * LANES, (c + 1) * LANES)
        y = xbuf_ref[CONV_HALO:CONV_HALO + tt, cs] * cw_ref[GDN_CONV - 1:GDN_CONV, cs]
        for i in range(GDN_CONV - 1):
            r0 = CONV_HALO - (GDN_CONV - 1) + i
            y = y + xbuf_ref[r0:r0 + tt, cs] * cw_ref[i:i + 1, cs]
        y = y * jax.nn.sigmoid(y)
        if c < 2 * GDN_HEADS:
            scale = GDN_DK ** -0.5 if c < GDN_HEADS else 1.0
            y = y * (lax.rsqrt(jnp.sum(y * y, axis=-1, keepdims=True) + L2_EPS) * scale)
        qkv_ref[0, :, cs] = y
    ba = h_ref[0, :, GDN_QKV + GDN_HEADS * GDN_DV:GDN_N]
    lane = lax.broadcasted_iota(jnp.int32, ba.shape, 1)
    xs = ba + dt_ref[...]
    softplus = jnp.maximum(xs, 0.0) + jnp.log(1.0 + jnp.exp(-jnp.abs(xs)))
    g = -jnp.exp(alog_ref[...]) * softplus
    bg_ref[0] = jnp.where(lane < GDN_HEADS, jax.nn.sigmoid(ba), g)


def gdn_prep(h, conv_buf, conv_w, a_log, dt_bias):
    b, t, _ = h.shape
    tt = min(t, 256)
    cbuf = jnp.pad(conv_buf, ((0, 0), (CONV_HALO - (GDN_CONV - 1), 0), (0, 0)))
    pad8 = lambda v: jnp.pad(v, (GDN_HEADS, LANES - 2 * GDN_HEADS)).reshape(1, LANES)
    hb = tt // CONV_HALO
    return pl.pallas_call(
        functools.partial(_gdn_prep_kernel, tt=tt),
        out_shape=(jax.ShapeDtypeStruct((b, t, GDN_QKV), F32), jax.ShapeDtypeStruct((b, t, LANES), F32)),
        grid=(b, t // tt),
        in_specs=[pl.BlockSpec((1, tt, GDN_N), lambda i, j: (i, j, 0)),
                  pl.BlockSpec((1, CONV_HALO, GDN_QKV), lambda i, j: (i, jnp.maximum(j * hb - 1, 0), 0)),
                  pl.BlockSpec((1, CONV_HALO, GDN_QKV), lambda i, j: (i, 0, 0)),
                  _resident((GDN_CONV, GDN_QKV)), _resident((1, LANES)), _resident((1, LANES))],
        out_specs=(pl.BlockSpec((1, tt, GDN_QKV), lambda i, j: (i, j, 0)),
                   pl.BlockSpec((1, tt, LANES), lambda i, j: (i, j, 0))),
        scratch_shapes=[pltpu.VMEM((CONV_HALO + tt, GDN_QKV), F32)],
        compiler_params=_compiler_params(2),
        name="gdn_prep",
    )(h, h, cbuf, conv_w, pad8(a_log), pad8(dt_bias))


def _dot(a, b):
    return jnp.dot(a.astype(BF16), b.astype(BF16), preferred_element_type=F32)


def _dot_nt(a, b):
    return lax.dot_general(a.astype(BF16), b.astype(BF16), (((1,), (1,)), ((), ())), preferred_element_type=F32)


def _dot_tn(a, b):
    return lax.dot_general(a.astype(BF16), b.astype(BF16), (((0,), (0,)), ((), ())), preferred_element_type=F32)


def _dot_f32(a, b):
    return jnp.dot(a, b, precision=lax.Precision.HIGHEST, preferred_element_type=F32)


def _gdn_scan_kernel(q_ref, k_ref, v_ref, bg_ref, z_ref, s0_ref, nw_ref, o_ref, s_ref, *, c):
    @pl.when(pl.program_id(1) == 0)
    def _():
        s_ref[...] = s0_ref[...]

    row = lax.broadcasted_iota(jnp.int32, (c, c), 0)
    col = lax.broadcasted_iota(jnp.int32, (c, c), 1)
    incl = row >= col
    strict = row > col
    tri = incl.astype(F32)
    bg = bg_ref[0]
    nw = nw_ref[...]
    n_double = c.bit_length() - 2
    for h in range(GDN_HEADS):
        hs = slice(h * GDN_DK, (h + 1) * GDN_DK)
        q = q_ref[0, :, hs]
        k = k_ref[0, :, hs]
        v = v_ref[0, :, hs]
        beta = bg[:, h:h + 1]
        g = bg[:, GDN_HEADS + h:GDN_HEADS + h + 1]
        diff = _dot_f32(tri, jnp.where(strict, g, 0.0))
        gb = _dot_f32(tri, jnp.broadcast_to(g, (c, GDN_DK)))
        decay = jnp.where(incl, jnp.exp(diff), 0.0)
        eg = jnp.exp(gb)
        g_last = gb[c - 1:c, :]
        tail = jnp.exp(g_last - gb)
        kk = _dot_nt(k, k)
        qk = _dot_nt(q, k)
        lmat = jnp.where(strict, beta * kk * decay, 0.0)
        x = -lmat
        p = lmat
        for _ in range(n_double):
            p = _dot(p, p)
            x = x + p + _dot(x, p)
        r = jnp.concatenate([beta * v, (beta * eg) * k], axis=1)
        tr = r + _dot(x, r)
        u1 = tr[:, :GDN_DV]
        w = tr[:, GDN_DV:]
        s = s_ref[0, h]
        u = u1 - _dot(w, s)
        o = eg * _dot(q, s) + _dot(qk * decay, u)
        s_ref[0, h] = jnp.exp(g_last) * s + _dot_tn(k * tail, u)
        on = o * lax.rsqrt(jnp.mean(o * o, axis=-1, keepdims=True) + RMS_EPS) * nw
        z = z_ref[0, :, hs]
        o_ref[0, :, hs] = on * (z * jax.nn.sigmoid(z))


def gdn_scan(qkv, bg, h, s0, norm_w, chunk):
    b, t, _ = qkv.shape
    assert chunk >= 8 and chunk & (chunk - 1) == 0 and t % chunk == 0
    hd = GDN_HEADS * GDN_DK
    blk = lambda col: pl.BlockSpec((1, chunk, hd), lambda i, j: (i, j, col))
    return pl.pallas_call(
        functools.partial(_gdn_scan_kernel, c=chunk),
        out_shape=(jax.ShapeDtypeStruct((b, t, GDN_HEADS * GDN_DV), F32),
                   jax.ShapeDtypeStruct((b, GDN_HEADS, GDN_DK, GDN_DV), F32)),
        grid=(b, t // chunk),
        in_specs=[blk(0), blk(1), blk(2), pl.BlockSpec((1, chunk, LANES), lambda i, j: (i, j, 0)), blk(3),
                  pl.BlockSpec((1, GDN_HEADS, GDN_DK, GDN_DV), lambda i, j: (i, 0, 0, 0)), _resident((1, GDN_DV))],
        out_specs=(pl.BlockSpec((1, chunk, GDN_HEADS * GDN_DV), lambda i, j: (i, j, 0)),
                   pl.BlockSpec((1, GDN_HEADS, GDN_DK, GDN_DV), lambda i, j: (i, 0, 0, 0))),
        compiler_params=_compiler_params(2),
        name="gdn_scan",
    )(qkv, qkv, qkv, bg, h, s0, norm_w.reshape(1, GDN_DV))


def gdn_mixer(h, s0, conv_buf, chunk, conv_w, a_log, dt_bias, norm_w):
    t = h.shape[1]
    qkv, bg = gdn_prep(h, conv_buf, conv_w, a_log, dt_bias)
    o, s = gdn_scan(qkv, bg, h, s0, norm_w, chunk)
    new_buf = jnp.concatenate([conv_buf, h[:, max(t - (GDN_CONV - 1), 0):, :GDN_QKV]], axis=1)[:, -(GDN_CONV - 1):]
    return o, s, new_buf


NSA_N = _round_up(NSA_IN, LANES)
NSA_QD = NSA_HEADS * NSA_DH
SEL_TILE = 256
assert CMP_LEN == 2 * CMP_STRIDE


def _nsa_compress_kernel(x_ref, pe_ref, w1_ref, w2_ref, o_ref, *, nblk):
    acc_a = jnp.zeros((nblk, CMP_HID), F32)
    acc_b = jnp.zeros((nblk, CMP_HID), F32)
    for l in range(CMP_STRIDE):
        xl = x_ref[0, 0, 0, pl.ds(l, nblk, stride=CMP_STRIDE), :]
        acc_a = acc_a + _dot(xl + pe_ref[0, l:l + 1, :], w1_ref[0, l])
        acc_b = acc_b + _dot(xl + pe_ref[0, CMP_STRIDE + l:CMP_STRIDE + l + 1, :], w1_ref[0, CMP_STRIDE + l])
    pre = acc_a + pltpu.roll(acc_b, nblk - 1, 0)
    hid = pre * jax.nn.sigmoid(pre)
    out = _dot(hid, w2_ref[0])
    row = lax.broadcasted_iota(jnp.int32, out.shape, 0)
    o_ref[0, 0, 0] = jnp.where(row < nblk - 1, out, 0.0)


def nsa_compress(x, pe, w1, w2):
    b, g, _, t, dh = x.shape
    nblk = t // CMP_STRIDE
    return pl.pallas_call(
        functools.partial(_nsa_compress_kernel, nblk=nblk),
        out_shape=jax.ShapeDtypeStruct((b, g, 2, nblk, dh), F32),
        grid=(b, g, 2),
        in_specs=[pl.BlockSpec((1, 1, 1, t, dh), lambda i, j, k: (i, j, k, 0, 0)),
                  pl.BlockSpec((1, CMP_LEN, dh), lambda i, j, k: (k, 0, 0)),
                  pl.BlockSpec((1, CMP_LEN, dh, CMP_HID), lambda i, j, k: (k, 0, 0, 0)),
                  pl.BlockSpec((1, CMP_HID, dh), lambda i, j, k: (k, 0, 0))],
        out_specs=pl.BlockSpec((1, 1, 1, nblk, dh), lambda i, j, k: (i, j, k, 0, 0)),
        compiler_params=_compiler_params(3),
        name="nsa_compress",
    )(x, pe, w1, w2)


def _masked_softmax(s, ok):
    s = jnp.where(ok, s, NEG_BIG)
    e = jnp.where(ok, jnp.exp(s - jnp.max(s, axis=-1, keepdims=True)), 0.0)
    return e / jnp.maximum(jnp.sum(e, axis=-1, keepdims=True), 1e-30)


def _select_blocks_t(psum, q_pos, n_sel):
    tq, ncp = psum.shape
    n_i = lax.broadcasted_iota(jnp.int32, (ncp, LANES), 0)
    j_i = lax.broadcasted_iota(jnp.int32, (ncp, LANES), 1)
    overlap = ((n_i >= 4 * j_i - 1) & (n_i <= 4 * j_i + 3) & (j_i < n_sel)).astype(F32)
    imp = _dot_f32(psum, overlap)
    jl = lax.broadcasted_iota(jnp.int32, (tq, LANES), 1)
    cur = jnp.right_shift(q_pos, 6)
    forced = (jl == 0) | (jl == cur) | (jl == cur - 1)
    score = jnp.where(jl > cur, -jnp.inf, jnp.where(forced, jnp.inf, imp))
    st = score.T[:n_sel]
    jrow = lax.broadcasted_iota(jnp.int32, st.shape, 0)
    cnt = jnp.zeros(st.shape, F32)
    for jp in range(n_sel):
        r = st[jp:jp + 1, :]
        ahead = (r > st) | ((r == st) & (jrow > jp))
        cnt = cnt + jnp.where(ahead, 1.0, 0.0)
    return jnp.where(cnt < SEL_TOP, 1.0, 0.0).astype(BF16)


def _nsa_prompt_kernel(q_ref, gate_ref, cmp_ref, ks_ref, vs_ref, kw_ref, vw_ref, o_ref, bias_ref, *, n_sel):
    i = pl.program_id(2)
    tq = Q_BLOCK
    ncp = cmp_ref.shape[3]
    q_pos = i * tq + lax.broadcasted_iota(jnp.int32, (tq, 1), 0)
    kc = cmp_ref[0, 0, 0].astype(BF16)
    vc = cmp_ref[0, 0, 1].astype(BF16)
    n_idx = lax.broadcasted_iota(jnp.int32, (tq, ncp), 1)
    ok_c = n_idx * CMP_STRIDE + (CMP_LEN - 1) <= q_pos
    gates = jax.nn.sigmoid(gate_ref[0, 0])
    qs = [(q_ref[0, :, h * NSA_DH:(h + 1) * NSA_DH] * NSA_DH ** -0.5).astype(BF16) for h in range(NSA_HPG)]

    psum = jnp.zeros((tq, ncp), F32)
    o_c = []
    for h in range(NSA_HPG):
        p = _masked_softmax(_dot_nt(qs[h], kc), ok_c)
        psum = psum + p
        o_c.append(_dot(p, vc))
    sel_t = _select_blocks_t(psum, q_pos, n_sel)

    n_tiles = (i * tq + tq + SEL_TILE - 1) // SEL_TILE

    def make_bias(kt, carry):
        jr = lax.broadcasted_iota(jnp.int32, (n_sel, SEL_TILE), 0)
        cl = lax.broadcasted_iota(jnp.int32, (n_sel, SEL_TILE), 1)
        expand = jnp.where(jr == kt * (SEL_TILE // SEL_BLOCK) + jnp.right_shift(cl, 6), 1.0, 0.0).astype(BF16)
        msel = lax.dot_general(sel_t, expand, (((0,), (0,)), ((), ())), preferred_element_type=F32)
        kpos = kt * SEL_TILE + lax.broadcasted_iota(jnp.int32, (tq, SEL_TILE), 1)
        bias_ref[kt] = jnp.where((msel > 0.5) & (kpos <= q_pos), 0.0, NEG_BIG)
        return carry

    lax.fori_loop(0, n_tiles, make_bias, 0)

    ws = pl.multiple_of(jnp.maximum(i * tq - WINDOW, 0), tq)
    wlen = WINDOW + tq
    wpos = ws + lax.broadcasted_iota(jnp.int32, (tq, wlen), 1)
    ok_w = (wpos <= q_pos) & (wpos >= q_pos - WINDOW)
    kw = kw_ref[0, 0, 0, pl.ds(ws, wlen), :]
    vw = vw_ref[0, 0, 0, pl.ds(ws, wlen), :]

    for h in range(NSA_HPG):
        qh = qs[h]

        def body(kt, carry):
            m, l, acc = carry
            k0 = pl.multiple_of(kt * SEL_TILE, SEL_TILE)
            s = _dot_nt(qh, ks_ref[0, 0, 0, pl.ds(k0, SEL_TILE), :]) + bias_ref[kt]
            m_new = jnp.maximum(m, jnp.max(s, axis=-1, keepdims=True))
            alpha = jnp.exp(m - m_new)
            p = jnp.exp(s - m_new)
            l = alpha * l + jnp.sum(p, axis=-1, keepdims=True)
            acc = alpha * acc + _dot(p, vs_ref[0, 0, 0, pl.ds(k0, SEL_TILE), :])
            return m_new, l, acc

        init = (jnp.full((tq, 1), NEG_BIG, F32), jnp.zeros((tq, 1), F32), jnp.zeros((tq, NSA_DH), F32))
        _, l, acc = lax.fori_loop(0, n_tiles, body, init)
        o_s = acc / l
        o_w = _dot(_masked_softmax(_dot_nt(qh, kw), ok_w), vw)
        gc = gates[:, 3 * h:3 * h + 1]
        gs = gates[:, 3 * h + 1:3 * h + 2]
        gw = gates[:, 3 * h + 2:3 * h + 3]
        o_ref[0, :, h * NSA_DH:(h + 1) * NSA_DH] = gc * o_c[h] + gs * o_s + gw * o_w


def nsa_prompt_attend(h, gate, cmp, kvw):
    b, t, _ = h.shape
    g = NSA_KV_GROUPS
    n_sel = t // SEL_BLOCK
    assert t % SEL_TILE == 0 and t >= WINDOW + Q_BLOCK and n_sel <= LANES and n_sel % 8 == 0
    gw = NSA_HPG * NSA_DH
    kv = lambda kind: pl.BlockSpec((1, 1, 1, t, NSA_DH), lambda i, j, k: (i, j, kind, 0, 0))
    return pl.pallas_call(
        functools.partial(_nsa_prompt_kernel, n_sel=n_sel),
        out_shape=jax.ShapeDtypeStruct((b, t, NSA_QD), F32),
        grid=(b, g, t // Q_BLOCK),
        in_specs=[pl.BlockSpec((1, Q_BLOCK, gw), lambda i, j, k: (i, k, j)),
                  pl.BlockSpec((1, 1, Q_BLOCK, LANES), lambda i, j, k: (i, j, k, 0)),
                  pl.BlockSpec((1, 1, 2, t // CMP_STRIDE, NSA_DH), lambda i, j, k: (i, j, 0, 0, 0)),
                  kv(0), kv(1), kv(2), kv(3)],
        out_specs=pl.BlockSpec((1, Q_BLOCK, gw), lambda i, j, k: (i, k, j)),
        scratch_shapes=[pltpu.VMEM((t // SEL_TILE, Q_BLOCK, SEL_TILE), F32)],
        compiler_params=_compiler_params(3),
        name="nsa_prompt_attend",
    )(h, gate, cmp, kvw, kvw, kvw, kvw)


def nsa_prompt(h, win_buf_len, pe_k, pe_v, w1k, w2k, w1v, w2v):
    b, t, _ = h.shape
    g, dh = NSA_KV_GROUPS, NSA_DH
    kv_rows = h[:, :, NSA_QD:NSA_QD + 4 * NSA_KV_DIM].reshape(b, t, 4, g, dh)
    win_rows = h[:, :, NSA_QD + 4 * NSA_KV_DIM:NSA_QD + 6 * NSA_KV_DIM].reshape(b, t, 2, g, dh)
    x_cmp = jnp.transpose(kv_rows[:, :, 0:2], (0, 3, 2, 1, 4))
    kvw = jnp.concatenate([jnp.transpose(kv_rows[:, :, 2:4], (0, 3, 2, 1, 4)),
                           jnp.transpose(win_rows, (0, 3, 2, 1, 4))], axis=2).astype(BF16)
    gate = h[:, :, NSA_QD + 6 * NSA_KV_DIM:NSA_IN].reshape(b, t, g, 3 * NSA_HPG)
    gate = jnp.pad(jnp.transpose(gate, (0, 2, 1, 3)), ((0, 0), (0, 0), (0, 0), (0, LANES - 3 * NSA_HPG)))
    pe = jnp.stack([pe_k, pe_v])
    w1 = jnp.stack([w1k, w1v]).reshape(2, CMP_LEN, dh, CMP_HID).astype(BF16)
    w2 = jnp.stack([w2k, w2v]).astype(BF16)
    cmp = nsa_compress(x_cmp, pe, w1, w2)
    o = nsa_prompt_attend(h, gate, cmp, kvw)
    wbuf = jnp.concatenate([jnp.zeros((b, win_buf_len, 2, g, dh), h.dtype), win_rows], axis=1)[:, t:]
    return o, kv_rows, wbuf


def rmsnorm(x, w):
    xf = x.astype(jnp.float32)
    y = xf * lax.rsqrt(jnp.mean(xf * xf, axis=-1, keepdims=True) + RMS_EPS)
    return (y * w.astype(jnp.float32)).astype(x.dtype)


def l2norm(x):
    xf = x.astype(jnp.float32)
    return xf * lax.rsqrt(jnp.sum(xf * xf, axis=-1, keepdims=True) + L2_EPS)


def masked_softmax(s, mask):
    p = jax.nn.softmax(jnp.where(mask, s, NEG_BIG), axis=-1)
    return jnp.where(mask, p, 0.0)


def nsa_split(h):
    B, T, _ = h.shape
    qd = NSA_HEADS * NSA_DH
    q = h[..., :qd].reshape(B, T, NSA_HEADS, NSA_DH) * NSA_DH ** -0.5
    kv = h[..., qd:qd + 4 * NSA_KV_DIM].reshape(B, T, 4, NSA_KV_GROUPS, NSA_DH)
    win = h[..., qd + 4 * NSA_KV_DIM:qd + 6 * NSA_KV_DIM].reshape(B, T, 2, NSA_KV_GROUPS, NSA_DH)
    gates = jax.nn.sigmoid(h[..., qd + 6 * NSA_KV_DIM:NSA_IN])
    return q, kv, win, gates


def compress_blocks(kv, pe, w1, w2):
    B, L = kv.shape[:2]
    n = (L - CMP_LEN) // CMP_STRIDE + 1
    idx = jnp.arange(n)[:, None] * CMP_STRIDE + jnp.arange(CMP_LEN)[None, :]
    blk = kv[:, idx] + pe[:, None, :]
    blk = jnp.moveaxis(blk, 3, 2).reshape(B, n, NSA_KV_GROUPS, CMP_LEN * NSA_DH)
    return jax.nn.silu(blk @ w1) @ w2


def selection_blocks(k):
    B, L = k.shape[:2]
    n_sel = -(-L // SEL_BLOCK)
    k = jnp.pad(k, ((0, 0), (0, n_sel * SEL_BLOCK - L), (0, 0), (0, 0)))
    return jnp.transpose(k.reshape(B, n_sel, SEL_BLOCK, NSA_KV_GROUPS, NSA_DH), (0, 3, 1, 2, 4))


def cmp_to_sel_overlap(n_cmp, n_sel):
    cs = jnp.arange(n_cmp)[:, None] * CMP_STRIDE
    ss = jnp.arange(n_sel)[None, :] * SEL_BLOCK
    return ((cs < ss + SEL_BLOCK) & (cs + CMP_LEN > ss)).astype(jnp.float32)


def branch_keys(kv_rows, pe_k, pe_v, w1k, w2k, w1v, w2v):
    kc = compress_blocks(kv_rows[:, :, 0], pe_k, w1k, w2k)
    vc = compress_blocks(kv_rows[:, :, 1], pe_v, w1v, w2v)
    k_sel = selection_blocks(kv_rows[:, :, 2])
    v_sel = selection_blocks(kv_rows[:, :, 3])
    overlap = cmp_to_sel_overlap(kc.shape[1], k_sel.shape[2])
    return kc, vc, k_sel, v_sel, overlap


def nsa_attend(q, q_pos, kc, vc, k_sel, v_sel, overlap, k_win, v_win, win_pos, gates):
    B, Tq = q.shape[:2]
    qg = q.reshape(B, Tq, NSA_KV_GROUPS, NSA_HPG, NSA_DH)
    cmp_end = jnp.arange(kc.shape[1]) * CMP_STRIDE + CMP_LEN - 1
    m_c = cmp_end[None, :] <= q_pos[:, None]
    s_c = jnp.einsum('bqghd,bngd->bghqn', qg, kc).astype(jnp.float32)
    p_c = masked_softmax(s_c, m_c)
    o_c = jnp.einsum('bghqn,bngd->bqghd', p_c.astype(vc.dtype), vc)
    n_sel = k_sel.shape[2]
    imp = jnp.einsum('bghqn,nj->bgqj', p_c, overlap)
    cur = q_pos // SEL_BLOCK
    j = jnp.arange(n_sel)[None, :]
    forced = (j == 0) | (j == cur[:, None]) | (j == cur[:, None] - 1)
    score = jnp.where(j > cur[:, None], -jnp.inf, jnp.where(forced, jnp.inf, imp))
    _, idx = lax.top_k(score, min(SEL_TOP, n_sel))
    n_top = idx.shape[-1]
    b_ix = jnp.arange(B)[:, None, None, None]
    g_ix = jnp.arange(NSA_KV_GROUPS)[None, :, None, None]
    ks = k_sel[b_ix, g_ix, idx].reshape(B, NSA_KV_GROUPS, Tq, n_top * SEL_BLOCK, NSA_DH)
    vs = v_sel[b_ix, g_ix, idx].reshape(B, NSA_KV_GROUPS, Tq, n_top * SEL_BLOCK, NSA_DH)
    pos = (idx[..., None] * SEL_BLOCK + jnp.arange(SEL_BLOCK)).reshape(B, NSA_KV_GROUPS, Tq, n_top * SEL_BLOCK)
    m_s = (pos <= q_pos[:, None])[:, :, None]
    s_s = jnp.einsum('bqghd,bgqkd->bghqk', qg, ks).astype(jnp.float32)
    p_s = masked_softmax(s_s, m_s)
    o_s = jnp.einsum('bghqk,bgqkd->bqghd', p_s.astype(vs.dtype), vs)
    m_w = (win_pos[None, :] <= q_pos[:, None]) & (win_pos[None, :] >= q_pos[:, None] - WINDOW) & (win_pos[None, :] >= 0)
    s_w = jnp.einsum('bqghd,bkgd->bghqk', qg, k_win).astype(jnp.float32)
    p_w = masked_softmax(s_w, m_w)
    o_w = jnp.einsum('bghqk,bkgd->bqghd', p_w.astype(v_win.dtype), v_win)
    g = gates.reshape(B, Tq, NSA_KV_GROUPS, NSA_HPG, 3)
    o = g[..., 0:1] * o_c + g[..., 1:2] * o_s + g[..., 2:3] * o_w
    return o.reshape(B, Tq, NSA_HEADS * NSA_DH)


def nsa_sample_core(h, kv_pool, win_cache, page_table, pe_k, pe_v, w1k, w2k, w1v, w2v):
    B, T, _ = h.shape
    q, kv_new, win_new, gates = nsa_split(h)
    past = kv_pool[page_table]
    past = past.reshape((B, -1) + past.shape[3:])
    past_len = past.shape[1]
    kv_all = jnp.concatenate([past, kv_new], axis=1)
    kc, vc, k_sel, v_sel, overlap = branch_keys(kv_all, pe_k, pe_v, w1k, w2k, w1v, w2v)
    win_all = jnp.concatenate([win_cache, win_new], axis=1)
    win_buf_len = win_cache.shape[1]
    win_pos = past_len - win_buf_len + jnp.arange(win_buf_len + T)
    q_pos = past_len + jnp.arange(T)
    o = nsa_attend(q, q_pos, kc, vc, k_sel, v_sel, overlap, win_all[:, :, 0], win_all[:, :, 1], win_pos, gates)
    return o, kv_new, win_all[:, T:]


def _pad_cols(w, n):
    return jnp.pad(w, ((0, 0), (0, n - w.shape[1])))


def kernel(x_prompt, x_sample, cache_nsa_kv, cache_nsa_win, state_gdn, state_gdn_conv, page_table,
           norm1_w, norm2_w, final_norm_w, gdn_w_in, gdn_conv_w, gdn_A_log, gdn_dt_bias, gdn_norm_w, gdn_w_out,
           nsa_w_in, nsa_pe_k, nsa_pe_v, nsa_cmp_w1_k, nsa_cmp_w2_k, nsa_cmp_w1_v, nsa_cmp_w2_v, nsa_w_out,
           ffn_w_in, ffn_w_out):
    bp, tp, d = x_prompt.shape
    bs, ts, _ = x_sample.shape
    mp, ms = bp * tp, bs * ts
    win_buf_len = cache_nsa_win.shape[2]
    x = jnp.concatenate([x_prompt.reshape(mp, d), x_sample.reshape(ms, d)], axis=0)
    gdn_n = _round_up(GDN_IN, LANES)
    nsa_n = _round_up(NSA_IN, LANES)
    kv_p, kv_s, win_p, win_s = [], [], [], []
    gs_p, gs_s, gc_p, gc_s = [], [], [], []
    for i in range(DEPTH):
        li = i // N_MIXERS
        if i % N_MIXERS == 0:
            h = norm_proj(x, norm1_w[i], _pad_cols(gdn_w_in[li], gdn_n).astype(BF16))
            hp = h[:mp].reshape(bp, tp, gdn_n)
            hs = h[mp:].reshape(bs, ts, gdn_n)
            gw = (gdn_conv_w[li], gdn_A_log[li], gdn_dt_bias[li], gdn_norm_w[li])
            s0 = jnp.zeros((bp, GDN_HEADS, GDN_DK, GDN_DV), F32)
            c0 = jnp.zeros((bp, GDN_CONV - 1, GDN_QKV), F32)
            op, sp, cp = gdn_mixer(hp, s0, c0, min(GDN_CHUNK, tp), *gw)
            os_, ss, cs = gdn_mixer(hs, state_gdn[li], state_gdn_conv[li], ts, *gw)
            gs_p.append(sp)
            gs_s.append(ss)
            gc_p.append(cp)
            gc_s.append(cs)
            o = jnp.concatenate([op.reshape(mp, -1), os_.reshape(ms, -1)], axis=0)
            x = proj_residual(x, o, gdn_w_out[li].astype(BF16))
        else:
            h = norm_proj(x, norm1_w[i], _pad_cols(nsa_w_in[li], nsa_n).astype(BF16))
            hp = h[:mp].reshape(bp, tp, nsa_n)
            hs = h[mp:].reshape(bs, ts, nsa_n)
            nw = (nsa_pe_k[li], nsa_pe_v[li], nsa_cmp_w1_k[li], nsa_cmp_w2_k[li], nsa_cmp_w1_v[li], nsa_cmp_w2_v[li])
            op, kp, wp = nsa_prompt(hp, win_buf_len, *nw)
            os_, ksm, wsm = nsa_sample_core(hs, cache_nsa_kv[li], cache_nsa_win[li], page_table, *nw)
            kv_p.append(kp)
            kv_s.append(ksm)
            win_p.append(wp)
            win_s.append(wsm)
            o = jnp.concatenate([op.reshape(mp, -1), os_.reshape(ms, -1)], axis=0)
            x = proj_residual(x, o, nsa_w_out[li].astype(BF16))
        x = ffn(x, norm2_w[i], ffn_w_in[i].astype(BF16), ffn_w_out[i].astype(BF16), final_norm_w, i == DEPTH - 1)
    y_prompt = x[:mp].reshape(bp, tp, d)
    y_sample = x[mp:].reshape(bs, ts, d)
    return (y_prompt, y_sample, jnp.stack(kv_p), jnp.stack(kv_s), jnp.stack(win_p), jnp.stack(win_s),
            jnp.stack(gs_p), jnp.stack(gs_s), jnp.stack(gc_p), jnp.stack(gc_s))
```

```python
import functools

import jax
import jax.numpy as jnp
from jax import lax
from jax.experimental import pallas as pl
from jax.experimental.pallas import tpu as pltpu

D_MODEL = 1024
DEPTH = 4
PAGE_SIZE = 128
N_MIXERS = 2
GDN_HEADS = 8
GDN_DK = 128
GDN_DV = 128
GDN_CONV = 4
GDN_CHUNK = 64
GDN_QKV = GDN_HEADS * (2 * GDN_DK + GDN_DV)
GDN_IN = GDN_QKV + GDN_HEADS * GDN_DV + 2 * GDN_HEADS
NSA_HEADS = 16
NSA_KV_GROUPS = 2
NSA_HPG = NSA_HEADS // NSA_KV_GROUPS
NSA_DH = 64
NSA_KV_DIM = NSA_KV_GROUPS * NSA_DH
CMP_LEN = 32
CMP_STRIDE = 16
CMP_HID = 256
SEL_BLOCK = 64
SEL_TOP = 16
WINDOW = 512
Q_BLOCK = 128
NSA_IN = NSA_HEADS * NSA_DH + 6 * NSA_KV_DIM + 3 * NSA_HEADS
D_FF = ((-(-8 * D_MODEL // 3) + 255) // 256) * 256
RMS_EPS = 1e-6
L2_EPS = 1e-6
NEG_BIG = -1e30

LANES = 128
VMEM_LIMIT = 56 * 1024 * 1024
ROW_TILE = 512
FF_CHUNK = 256

BF16 = jnp.bfloat16
F32 = jnp.float32


def _round_up(n, m):
    return -(-n // m) * m


def _rms_rows(x, w):
    return x * lax.rsqrt(jnp.mean(x * x, axis=-1, keepdims=True) + RMS_EPS) * w


def _compiler_params(n_axes):
    return pltpu.CompilerParams(dimension_semantics=("arbitrary",) * n_axes, vmem_limit_bytes=VMEM_LIMIT)


def _resident(shape):
    return pl.BlockSpec(shape, lambda *_: (0,) * len(shape), pipeline_mode=pl.Buffered(1))


def _norm_proj_kernel(x_ref, nw_ref, w_ref, o_ref):
    xn = _rms_rows(x_ref[...], nw_ref[...]).astype(BF16)
    n = w_ref.shape[1]
    for c0 in range(0, n, 512):
        c1 = min(c0 + 512, n)
        o_ref[:, c0:c1] = jnp.dot(xn, w_ref[:, c0:c1], preferred_element_type=F32)


def norm_proj(x, nw, w):
    m, d = x.shape
    n = w.shape[1]
    tm = 256
    return pl.pallas_call(
        _norm_proj_kernel,
        out_shape=jax.ShapeDtypeStruct((m, n), F32),
        grid=(m // tm,),
        in_specs=[pl.BlockSpec((tm, d), lambda i: (i, 0)), _resident((1, d)), _resident((d, n))],
        out_specs=pl.BlockSpec((tm, n), lambda i: (i, 0)),
        compiler_params=_compiler_params(1),
        name="norm_proj",
    )(x, nw.reshape(1, d), w)


def _proj_residual_kernel(x_ref, a_ref, w_ref, o_ref):
    o_ref[...] = x_ref[...] + jnp.dot(a_ref[...].astype(BF16), w_ref[...], preferred_element_type=F32)


def proj_residual(x, a, w):
    m, d = x.shape
    k = a.shape[1]
    tm = ROW_TILE
    return pl.pallas_call(
        _proj_residual_kernel,
        out_shape=jax.ShapeDtypeStruct((m, d), F32),
        grid=(m // tm,),
        in_specs=[pl.BlockSpec((tm, d), lambda i: (i, 0)), pl.BlockSpec((tm, k), lambda i: (i, 0)), _resident((k, d))],
        out_specs=pl.BlockSpec((tm, d), lambda i: (i, 0)),
        compiler_params=_compiler_params(1),
        name="proj_residual",
    )(x, a, w)


def _ffn_kernel(x_ref, nw_ref, wi_ref, wo_ref, fw_ref, o_ref, act_ref, *, final_norm):
    x = x_ref[...]
    xn = _rms_rows(x, nw_ref[...]).astype(BF16)
    for c0 in range(0, D_FF, FF_CHUNK):
        g = jnp.dot(xn, wi_ref[:, c0:c0 + FF_CHUNK], preferred_element_type=F32)
        u = jnp.dot(xn, wi_ref[:, D_FF + c0:D_FF + c0 + FF_CHUNK], preferred_element_type=F32)
        act_ref[:, c0:c0 + FF_CHUNK] = (g * jax.nn.sigmoid(g) * u).astype(BF16)
    y = x + jnp.dot(act_ref[...], wo_ref[...], preferred_element_type=F32)
    if final_norm:
        y = _rms_rows(y, fw_ref[...])
    o_ref[...] = y


def ffn(x, nw, w_in, w_out, final_w, final_norm):
    m, d = x.shape
    tm = ROW_TILE
    return pl.pallas_call(
        functools.partial(_ffn_kernel, final_norm=final_norm),
        out_shape=jax.ShapeDtypeStruct((m, d), F32),
        grid=(m // tm,),
        in_specs=[pl.BlockSpec((tm, d), lambda i: (i, 0)), _resident((1, d)), _resident((d, 2 * D_FF)),
                  _resident((D_FF, d)), _resident((1, d))],
        out_specs=pl.BlockSpec((tm, d), lambda i: (i, 0)),
        scratch_shapes=[pltpu.VMEM((tm, D_FF), BF16)],
        compiler_params=_compiler_params(1),
        name="ffn",
    )(x, nw.reshape(1, d), w_in, w_out, final_w.reshape(1, d))


GDN_N = _round_up(GDN_IN, LANES)
CONV_HALO = 8


def _gdn_prep_kernel(h_ref, halo_ref, cbuf_ref, cw_ref, alog_ref, dt_ref, qkv_ref, bg_ref, xbuf_ref, *, tt):
    t = pl.program_id(1)
    prev = jnp.where(t == 0, cbuf_ref[0], halo_ref[0])
    xbuf_ref[0:CONV_HALO, :] = prev
    xbuf_ref[CONV_HALO:CONV_HALO + tt, :] = h_ref[0, :, 0:GDN_QKV]
    for c in range(GDN_QKV // LANES):
        cs = slice(c * LANES, (c + 1) * LANES)
        y = xbuf_ref[CONV_HALO:CONV_HALO + tt, cs] * cw_ref[GDN_CONV - 1:GDN_CONV, cs]
        for i in range(GDN_CONV - 1):
            r0 = CONV_HALO - (GDN_CONV - 1) + i
            y = y + xbuf_ref[r0:r0 + tt, cs] * cw_ref[i:i + 1, cs]
        y = y * jax.nn.sigmoid(y)
        if c < 2 * GDN_HEADS:
            scale = GDN_DK ** -0.5 if c < GDN_HEADS else 1.0
            y = y * (lax.rsqrt(jnp.sum(y * y, axis=-1, keepdims=True) + L2_EPS) * scale)
        qkv_ref[0, :, cs] = y
    ba = h_ref[0, :, GDN_QKV + GDN_HEADS * GDN_DV:GDN_N]
    lane = lax.broadcasted_iota(jnp.int32, ba.shape, 1)
    xs = ba + dt_ref[...]
    softplus = jnp.maximum(xs, 0.0) + jnp.log(1.0 + jnp.exp(-jnp.abs(xs)))
    g = -jnp.exp(alog_ref[...]) * softplus
    bg_ref[0] = jnp.where(lane < GDN_HEADS, jax.nn.sigmoid(ba), g)


def gdn_prep(h, conv_buf, conv_w, a_log, dt_bias):
    b, t, _ = h.shape
    tt = min(t, 256)
    cbuf = jnp.pad(conv_buf, ((0, 0), (CONV_HALO - (GDN_CONV - 1), 0), (0, 0)))
    pad8 = lambda v: jnp.pad(v, (GDN_HEADS, LANES - 2 * GDN_HEADS)).reshape(1, LANES)
    hb = tt // CONV_HALO
    return pl.pallas_call(
        functools.partial(_gdn_prep_kernel, tt=tt),
        out_shape=(jax.ShapeDtypeStruct((b, t, GDN_QKV), F32), jax.ShapeDtypeStruct((b, t, LANES), F32)),
        grid=(b, t // tt),
        in_specs=[pl.BlockSpec((1, tt, GDN_N), lambda i, j: (i, j, 0)),
                  pl.BlockSpec((1, CONV_HALO, GDN_QKV), lambda i, j: (i, jnp.maximum(j * hb - 1, 0), 0)),
                  pl.BlockSpec((1, CONV_HALO, GDN_QKV), lambda i, j: (i, 0, 0)),
                  _resident((GDN_CONV, GDN_QKV)), _resident((1, LANES)), _resident((1, LANES))],
        out_specs=(pl.BlockSpec((1, tt, GDN_QKV), lambda i, j: (i, j, 0)),
                   pl.BlockSpec((1, tt, LANES), lambda i, j: (i, j, 0))),
        scratch_shapes=[pltpu.VMEM((CONV_HALO + tt, GDN_QKV), F32)],
        compiler_params=_compiler_params(2),
        name="gdn_prep",
    )(h, h, cbuf, conv_w, pad8(a_log), pad8(dt_bias))


def _dot(a, b):
    return jnp.dot(a.astype(BF16), b.astype(BF16), preferred_element_type=F32)


def _dot_nt(a, b):
    return lax.dot_general(a.astype(BF16), b.astype(BF16), (((1,), (1,)), ((), ())), preferred_element_type=F32)


def _dot_tn(a, b):
    return lax.dot_general(a.astype(BF16), b.astype(BF16), (((0,), (0,)), ((), ())), preferred_element_type=F32)


def _dot_f32(a, b):
    return jnp.dot(a, b, precision=lax.Precision.HIGHEST, preferred_element_type=F32)


def _gdn_scan_kernel(q_ref, k_ref, v_ref, bg_ref, z_ref, s0_ref, nw_ref, o_ref, s_ref, *, c):
    @pl.when(pl.program_id(1) == 0)
    def _():
        s_ref[...] = s0_ref[...]

    row = lax.broadcasted_iota(jnp.int32, (c, c), 0)
    col = lax.broadcasted_iota(jnp.int32, (c, c), 1)
    incl = row >= col
    strict = row > col
    tri = incl.astype(F32)
    bg = bg_ref[0]
    nw = nw_ref[...]
    n_double = c.bit_length() - 2
    for h in range(GDN_HEADS):
        hs = slice(h * GDN_DK, (h + 1) * GDN_DK)
        q = q_ref[0, :, hs]
        k = k_ref[0, :, hs]
        v = v_ref[0, :, hs]
        beta = bg[:, h:h + 1]
        g = bg[:, GDN_HEADS + h:GDN_HEADS + h + 1]
        diff = _dot_f32(tri, jnp.where(strict, g, 0.0))
        gb = _dot_f32(tri, jnp.broadcast_to(g, (c, GDN_DK)))
        decay = jnp.where(incl, jnp.exp(diff), 0.0)
        eg = jnp.exp(gb)
        g_last = gb[c - 1:c, :]
        tail = jnp.exp(g_last - gb)
        kk = _dot_nt(k, k)
        qk = _dot_nt(q, k)
        lmat = jnp.where(strict, beta * kk * decay, 0.0)
        x = -lmat
        p = lmat
        for _ in range(n_double):
            p = _dot(p, p)
            x = x + p + _dot(x, p)
        r = jnp.concatenate([beta * v, (beta * eg) * k], axis=1)
        tr = r + _dot(x, r)
        u1 = tr[:, :GDN_DV]
        w = tr[:, GDN_DV:]
        s = s_ref[0, h]
        u = u1 - _dot(w, s)
        o = eg * _dot(q, s) + _dot(qk * decay, u)
        s_ref[0, h] = jnp.exp(g_last) * s + _dot_tn(k * tail, u)
        on = o * lax.rsqrt(jnp.mean(o * o, axis=-1, keepdims=True) + RMS_EPS) * nw
        z = z_ref[0, :, hs]
        o_ref[0, :, hs] = on * (z * jax.nn.sigmoid(z))


def gdn_scan(qkv, bg, h, s0, norm_w, chunk):
    b, t, _ = qkv.shape
    assert chunk >= 8 and chunk & (chunk - 1) == 0 and t % chunk == 0
    hd = GDN_HEADS * GDN_DK
    blk = lambda col: pl.BlockSpec((1, chunk, hd), lambda i, j: (i, j, col))
    return pl.pallas_call(
        functools.partial(_gdn_scan_kernel, c=chunk),
        out_shape=(jax.ShapeDtypeStruct((b, t, GDN_HEADS * GDN_DV), F32),
                   jax.ShapeDtypeStruct((b, GDN_HEADS, GDN_DK, GDN_DV), F32)),
        grid=(b, t // chunk),
        in_specs=[blk(0), blk(1), blk(2), pl.BlockSpec((1, chunk, LANES), lambda i, j: (i, j, 0)), blk(3),
                  pl.BlockSpec((1, GDN_HEADS, GDN_DK, GDN_DV), lambda i, j: (i, 0, 0, 0)), _resident((1, GDN_DV))],
        out_specs=(pl.BlockSpec((1, chunk, GDN_HEADS * GDN_DV), lambda i, j: (i, j, 0)),
                   pl.BlockSpec((1, GDN_HEADS, GDN_DK, GDN_DV), lambda i, j: (i, 0, 0, 0))),
        compiler_params=_compiler_params(2),
        name="gdn_scan",
    )(qkv, qkv, qkv, bg, h, s0, norm_w.reshape(1, GDN_DV))


def gdn_mixer(h, s0, conv_buf, chunk, conv_w, a_log, dt_bias, norm_w):
    t = h.shape[1]
    qkv, bg = gdn_prep(h, conv_buf, conv_w, a_log, dt_bias)
    o, s = gdn_scan(qkv, bg, h, s0, norm_w, chunk)
    new_buf = jnp.concatenate([conv_buf, h[:, max(t - (GDN_CONV - 1), 0):, :GDN_QKV]], axis=1)[:, -(GDN_CONV - 1):]
    return o, s, new_buf


NSA_N = _round_up(NSA_IN, LANES)
NSA_QD = NSA_HEADS * NSA_DH
SEL_TILE = 256
assert CMP_LEN == 2 * CMP_STRIDE


def _nsa_compress_kernel(x_ref, pe_ref, w1_ref, w2_ref, o_ref, *, nblk):
    acc_a = jnp.zeros((nblk, CMP_HID), F32)
    acc_b = jnp.zeros((nblk, CMP_HID), F32)
    for l in range(CMP_STRIDE):
        xl = x_ref[0, 0, 0, pl.ds(l, nblk, stride=CMP_STRIDE), :]
        acc_a = acc_a + _dot(xl + pe_ref[0, l:l + 1, :], w1_ref[0, l])
        acc_b = acc_b + _dot(xl + pe_ref[0, CMP_STRIDE + l:CMP_STRIDE + l + 1, :], w1_ref[0, CMP_STRIDE + l])
    pre = acc_a + pltpu.roll(acc_b, nblk - 1, 0)
    hid = pre * jax.nn.sigmoid(pre)
    out = _dot(hid, w2_ref[0])
    row = lax.broadcasted_iota(jnp.int32, out.shape, 0)
    o_ref[0, 0, 0] = jnp.where(row < nblk - 1, out, 0.0)


def nsa_compress(x, pe, w1, w2):
    b, g, _, t, dh = x.shape
    nblk = t // CMP_STRIDE
    return pl.pallas_call(
        functools.partial(_nsa_compress_kernel, nblk=nblk),
        out_shape=jax.ShapeDtypeStruct((b, g, 2, nblk, dh), F32),
        grid=(b, g, 2),
        in_specs=[pl.BlockSpec((1, 1, 1, t, dh), lambda i, j, k: (i, j, k, 0, 0)),
                  pl.BlockSpec((1, CMP_LEN, dh), lambda i, j, k: (k, 0, 0)),
                  pl.BlockSpec((1, CMP_LEN, dh, CMP_HID), lambda i, j, k: (k, 0, 0, 0)),
                  pl.BlockSpec((1, CMP_HID, dh), lambda i, j, k: (k, 0, 0))],
        out_specs=pl.BlockSpec((1, 1, 1, nblk, dh), lambda i, j, k: (i, j, k, 0, 0)),
        compiler_params=_compiler_params(3),
        name="nsa_compress",
    )(x, pe, w1, w2)


WIN_TILE = 128
V_ROWS = NSA_DH + 16


def _heads_on_lanes(x, n_heads):
    return jnp.concatenate([x] * n_heads, axis=1)


def _select_blocks(psum_t, q_pos, n_sel):
    ncp, tq = psum_t.shape
    n_rows = _round_up(n_sel, 8)
    j_i = lax.broadcasted_iota(jnp.int32, (LANES, ncp), 0)
    n_i = lax.broadcasted_iota(jnp.int32, (LANES, ncp), 1)
    overlap_t = ((n_i >= 4 * j_i - 1) & (n_i <= 4 * j_i + 3) & (j_i < n_sel)).astype(F32)
    imp = _dot_f32(overlap_t, psum_t)[:n_rows]
    jrow = lax.broadcasted_iota(jnp.int32, (n_rows, tq), 0)
    cur = jnp.right_shift(q_pos, 6)
    forced = (jrow == 0) | (jrow == cur) | (jrow == cur - 1)
    st = jnp.where(jrow > cur, -jnp.inf, jnp.where(forced, jnp.inf, imp))
    cnt = jnp.zeros(st.shape, F32)
    for jp in range(n_sel):
        r = st[jp:jp + 1, :]
        ahead = (r > st) | ((r == st) & (jrow > jp))
        cnt = cnt + jnp.where(ahead, 1.0, 0.0)
    return jnp.where(cnt < SEL_TOP, 1.0, 0.0).astype(BF16)


def _nsa_prompt_kernel(q_ref, gate_ref, cmp_ref, ks_ref, vs_ref, kw_ref, vw_ref, o_ref, *, n_sel):
    i = pl.program_id(2)
    tq, nh, dh = Q_BLOCK, NSA_HPG, NSA_DH
    ncp = cmp_ref.shape[3]
    q_pos = i * tq + lax.broadcasted_iota(jnp.int32, (1, tq), 1)
    q_t = jnp.concatenate([(q_ref[0, :, h * dh:(h + 1) * dh] * dh ** -0.5).T for h in range(nh)], axis=1).astype(BF16)

    kc = cmp_ref[0, 0, 0].astype(BF16)
    vc_t = cmp_ref[0, 0, 1].T.astype(BF16)
    n_idx = lax.broadcasted_iota(jnp.int32, (ncp, tq), 0)
    ok_c = _heads_on_lanes(n_idx * CMP_STRIDE + (CMP_LEN - 1) <= q_pos, nh)
    s = jnp.where(ok_c, _dot(kc, q_t), NEG_BIG)
    e = jnp.where(ok_c, jnp.exp(s - jnp.max(s, axis=0, keepdims=True)), 0.0)
    p_c = e / jnp.maximum(jnp.sum(e, axis=0, keepdims=True), 1e-30)
    o_c = _dot(vc_t, p_c)
    psum = p_c[:, 0:tq]
    for h in range(1, nh):
        psum = psum + p_c[:, h * tq:(h + 1) * tq]
    sel = _select_blocks(psum, q_pos, n_sel)

    n_tiles = (i * tq + tq + SEL_TILE - 1) // SEL_TILE

    def body(kt, carry):
        m, acc = carry
        cr = lax.broadcasted_iota(jnp.int32, (SEL_TILE, n_sel), 0)
        jc = lax.broadcasted_iota(jnp.int32, (SEL_TILE, n_sel), 1)
        expand = jnp.where(jc == kt * (SEL_TILE // SEL_BLOCK) + jnp.right_shift(cr, 6), 1.0, 0.0).astype(BF16)
        kpos = kt * SEL_TILE + lax.broadcasted_iota(jnp.int32, (SEL_TILE, tq), 0)
        ok = (_dot(expand, sel) > 0.5) & (kpos <= q_pos)
        bias = _heads_on_lanes(jnp.where(ok, 0.0, NEG_BIG), nh)
        k0 = pl.multiple_of(kt * SEL_TILE, SEL_TILE)
        s = _dot(ks_ref[0, 0, 0, pl.ds(k0, SEL_TILE), :], q_t) + bias
        m_new = jnp.maximum(m, jnp.max(s, axis=0, keepdims=True))
        p = jnp.exp(s - m_new)
        acc = jnp.exp(m - m_new) * acc + _dot(vs_ref[0, 0, 0, kt], p)
        return m_new, acc

    init = (jnp.full((1, nh * tq), NEG_BIG, F32), jnp.zeros((V_ROWS, nh * tq), F32))
    _, acc = lax.fori_loop(0, n_tiles, body, init)
    o_s = acc[0:dh] / acc[dh:dh + 1]

    wlen = WINDOW + tq
    ws = pl.multiple_of(jnp.maximum(i * tq - WINDOW, 0), WIN_TILE)
    wpos = ws + lax.broadcasted_iota(jnp.int32, (wlen, tq), 0)
    ok_w = _heads_on_lanes((wpos <= q_pos) & (wpos >= q_pos - WINDOW), nh)
    s = jnp.where(ok_w, _dot(kw_ref[0, 0, 0, pl.ds(ws, wlen), :], q_t), NEG_BIG)
    p_w = jnp.exp(s - jnp.max(s, axis=0, keepdims=True)).astype(BF16)
    wt0 = ws // WIN_TILE
    acc_w = jnp.zeros((V_ROWS, nh * tq), F32)
    for t in range(wlen // WIN_TILE):
        acc_w = acc_w + jnp.dot(vw_ref[0, 0, 0, wt0 + t], p_w[t * WIN_TILE:(t + 1) * WIN_TILE],
                                preferred_element_type=F32)
    o_w = acc_w[0:dh] / acc_w[dh:dh + 1]

    g_t = jax.nn.sigmoid(gate_ref[0, 0]).T
    gate_row = lambda br: jnp.concatenate([g_t[3 * h + br:3 * h + br + 1, :] for h in range(nh)], axis=1)
    o_t = gate_row(0) * o_c + gate_row(1) * o_s + gate_row(2) * o_w
    o_ref[0] = jnp.concatenate([o_t[:, h * tq:(h + 1) * tq].T for h in range(nh)], axis=1)


def nsa_prompt_attend(h, gate, cmp, k_rows, vs_t, vw_t):
    b, t, _ = h.shape
    g = NSA_KV_GROUPS
    n_sel = t // SEL_BLOCK
    assert t % SEL_TILE == 0 and t >= WINDOW + Q_BLOCK and n_sel <= LANES and n_sel % 8 == 0
    gw = NSA_HPG * NSA_DH
    kr = lambda kind: pl.BlockSpec((1, 1, 1, t, NSA_DH), lambda i, j, k: (i, j, kind, 0, 0))
    vt = lambda tile: pl.BlockSpec((1, 1, 1, t // tile, V_ROWS, tile), lambda i, j, k: (i, j, 0, 0, 0, 0))
    return pl.pallas_call(
        functools.partial(_nsa_prompt_kernel, n_sel=n_sel),
        out_shape=jax.ShapeDtypeStruct((b, t, NSA_QD), F32),
        grid=(b, g, t // Q_BLOCK),
        in_specs=[pl.BlockSpec((1, Q_BLOCK, gw), lambda i, j, k: (i, k, j)),
                  pl.BlockSpec((1, 1, Q_BLOCK, LANES), lambda i, j, k: (i, j, k, 0)),
                  pl.BlockSpec((1, 1, 2, t // CMP_STRIDE, NSA_DH), lambda i, j, k: (i, j, 0, 0, 0)),
                  kr(0), vt(SEL_TILE), kr(1), vt(WIN_TILE)],
        out_specs=pl.BlockSpec((1, Q_BLOCK, gw), lambda i, j, k: (i, k, j)),
        compiler_params=_compiler_params(3),
        name="nsa_prompt_attend",
    )(h, gate, cmp, k_rows, vs_t, k_rows, vw_t)


def _value_tiles_t(v, tile):
    b, g, t, dh = v.shape
    vt = jnp.transpose(v.reshape(b, g, t // tile, tile, dh), (0, 1, 2, 4, 3))
    extra = jnp.zeros((b, g, t // tile, V_ROWS - dh, tile), v.dtype).at[:, :, :, 0, :].set(1.0)
    return jnp.concatenate([vt, extra], axis=3).astype(BF16)[:, :, None]


def nsa_prompt(h, win_buf_len, pe_k, pe_v, w1k, w2k, w1v, w2v):
    b, t, _ = h.shape
    g, dh = NSA_KV_GROUPS, NSA_DH
    kv_rows = h[:, :, NSA_QD:NSA_QD + 4 * NSA_KV_DIM].reshape(b, t, 4, g, dh)
    win_rows = h[:, :, NSA_QD + 4 * NSA_KV_DIM:NSA_QD + 6 * NSA_KV_DIM].reshape(b, t, 2, g, dh)
    by_group = lambda x: jnp.transpose(x, (0, 3, 2, 1, 4))
    x_cmp = by_group(kv_rows[:, :, 0:2])
    k_rows = by_group(jnp.stack([kv_rows[:, :, 2], win_rows[:, :, 0]], axis=2)).astype(BF16)
    vs_t = _value_tiles_t(jnp.transpose(kv_rows[:, :, 3], (0, 2, 1, 3)), SEL_TILE)
    vw_t = _value_tiles_t(jnp.transpose(win_rows[:, :, 1], (0, 2, 1, 3)), WIN_TILE)
    gate = h[:, :, NSA_QD + 6 * NSA_KV_DIM:NSA_IN].reshape(b, t, g, 3 * NSA_HPG)
    gate = jnp.pad(jnp.transpose(gate, (0, 2, 1, 3)), ((0, 0), (0, 0), (0, 0), (0, LANES - 3 * NSA_HPG)))
    pe = jnp.stack([pe_k, pe_v])
    w1 = jnp.stack([w1k, w1v]).reshape(2, CMP_LEN, dh, CMP_HID).astype(BF16)
    w2 = jnp.stack([w2k, w2v]).astype(BF16)
    cmp = nsa_compress(x_cmp, pe, w1, w2)
    o = nsa_prompt_attend(h, gate, cmp, k_rows, vs_t, vw_t)
    wbuf = jnp.concatenate([jnp.zeros((b, win_buf_len, 2, g, dh), h.dtype), win_rows], axis=1)[:, t:]
    return o, kv_rows, wbuf


def _nsa_compress_paged_kernel(pt_ref, *refs, n_pages):
    del pt_ref
    page_refs = (refs[:n_pages], refs[n_pages:2 * n_pages])
    pe_ref, w1_ref, w2_ref, o_ref = refs[2 * n_pages:]
    dh = NSA_DH
    rpp = PAGE_SIZE // CMP_STRIDE
    nblk = n_pages * rpp
    row = lax.broadcasted_iota(jnp.int32, (nblk, dh), 0)
    for kind in range(2):
        acc_a = [jnp.zeros((nblk, CMP_HID), F32) for _ in range(NSA_KV_GROUPS)]
        acc_b = [jnp.zeros((nblk, CMP_HID), F32) for _ in range(NSA_KV_GROUPS)]
        for l in range(CMP_STRIDE):
            xl = jnp.concatenate([r[0, pl.ds(l, rpp, stride=CMP_STRIDE), :] for r in page_refs[kind]], axis=0)
            for g in range(NSA_KV_GROUPS):
                x = xl[:, g * dh:(g + 1) * dh]
                acc_a[g] = acc_a[g] + _dot(x + pe_ref[kind, l:l + 1, :], w1_ref[kind, l])
                acc_b[g] = acc_b[g] + _dot(x + pe_ref[kind, CMP_STRIDE + l:CMP_STRIDE + l + 1, :],
                                           w1_ref[kind, CMP_STRIDE + l])
        for g in range(NSA_KV_GROUPS):
            pre = acc_a[g] + pltpu.roll(acc_b[g], nblk - 1, 0)
            hid = pre * jax.nn.sigmoid(pre)
            o_ref[0, g, kind] = jnp.where(row < nblk - 1, _dot(hid, w2_ref[kind]), 0.0)


def nsa_compress_paged(pool, page_table, pe, w1, w2):
    b, n_pages = page_table.shape
    nblk = n_pages * (PAGE_SIZE // CMP_STRIDE)
    page = lambda kind, p: pl.BlockSpec((1, PAGE_SIZE, NSA_KV_DIM), lambda i, pt: (pt[i, p], 0, kind))
    const = lambda shape: pl.BlockSpec(shape, lambda i, pt: (0,) * len(shape))
    return pl.pallas_call(
        functools.partial(_nsa_compress_paged_kernel, n_pages=n_pages),
        out_shape=jax.ShapeDtypeStruct((b, NSA_KV_GROUPS, 2, nblk, NSA_DH), F32),
        grid_spec=pltpu.PrefetchScalarGridSpec(
            num_scalar_prefetch=1, grid=(b,),
            in_specs=[page(kind, p) for kind in range(2) for p in range(n_pages)]
            + [const(pe.shape), const(w1.shape), const(w2.shape)],
            out_specs=pl.BlockSpec((1, NSA_KV_GROUPS, 2, nblk, NSA_DH), lambda i, pt: (i, 0, 0, 0, 0))),
        compiler_params=_compiler_params(1),
        name="nsa_compress_paged",
    )(page_table, *([pool] * (2 * n_pages)), pe, w1, w2)


def _nsa_sample_kernel(pt_ref, *refs, n_pages, ts, win_len):
    del pt_ref
    page_refs = refs[:n_pages]
    q_ref, gate_ref, kvn_ref, wn_ref, cmp_ref, win_ref, o_ref = refs[n_pages:]
    dh, nh, g_n = NSA_DH, NSA_HPG, NSA_KV_GROUPS
    nc = nh * ts
    past = n_pages * PAGE_SIZE
    n_sel = -(-(past + ts) // SEL_BLOCK)
    ncp = cmp_ref.shape[3]
    tcol = lax.broadcasted_iota(jnp.int32, (1, nc), 1) % ts
    q_pos = past + tcol
    cc = lax.broadcasted_iota(jnp.int32, (nc, LANES), 0) % ts
    tl = lax.broadcasted_iota(jnp.int32, (nc, LANES), 1)
    head_sum = (cc == tl).astype(F32)
    tr = lax.broadcasted_iota(jnp.int32, (LANES, nc), 0)
    ce = lax.broadcasted_iota(jnp.int32, (LANES, nc), 1) % ts
    head_expand = (tr == ce).astype(BF16)
    q_pos_l = past + jnp.minimum(lax.broadcasted_iota(jnp.int32, (1, LANES), 1), ts - 1)
    n_idx = lax.broadcasted_iota(jnp.int32, (ncp, nc), 0)
    ok_c = n_idx * CMP_STRIDE + (CMP_LEN - 1) <= q_pos
    key_in_page = lax.broadcasted_iota(jnp.int32, (PAGE_SIZE, nc), 0)
    t_new = lax.broadcasted_iota(jnp.int32, (ts, nc), 0)
    ok_new = t_new <= tcol
    widx = lax.broadcasted_iota(jnp.int32, (win_len, nc), 0)
    wpos = past - win_len + widx
    ok_wc = (wpos <= q_pos) & (wpos >= q_pos - WINDOW) & (wpos >= 0)

    for g in range(g_n):
        q_t = (q_ref[0, g] * dh ** -0.5).astype(BF16)
        kc = cmp_ref[0, g, 0].astype(BF16)
        vc = cmp_ref[0, g, 1]
        s = jnp.where(ok_c, _dot(kc, q_t), NEG_BIG)
        e = jnp.where(ok_c, jnp.exp(s - jnp.max(s, axis=0, keepdims=True)), 0.0)
        p_c = e / jnp.maximum(jnp.sum(e, axis=0, keepdims=True), 1e-30)
        o_c = _dot_tn(vc, p_c)
        sel = _select_blocks(_dot_f32(p_c, head_sum), q_pos_l, n_sel)
        sel_cols = _dot(sel, head_expand)
        s_tiles, v_tiles = [], []
        for p in range(n_pages):
            blk = page_refs[p][0]
            k = blk[:, g * dh:(g + 1) * dh]
            v_tiles.append(blk[:, NSA_KV_DIM + g * dh:NSA_KV_DIM + (g + 1) * dh])
            per_page = PAGE_SIZE // SEL_BLOCK
            okp = sel_cols[per_page * p:per_page * p + 1, :]
            for jb in range(1, per_page):
                okp = jnp.where(key_in_page < jb * SEL_BLOCK, okp, sel_cols[per_page * p + jb:per_page * p + jb + 1, :])
            s_tiles.append(jnp.where(okp > 0.5, _dot(k, q_t), NEG_BIG))
        kvn = kvn_ref[0]
        k_new = kvn[:, 2 * NSA_KV_DIM + g * dh:2 * NSA_KV_DIM + (g + 1) * dh]
        v_tiles.append(kvn[:, 3 * NSA_KV_DIM + g * dh:3 * NSA_KV_DIM + (g + 1) * dh])
        s_tiles.append(jnp.where(ok_new, _dot(k_new, q_t), NEG_BIG))
        m = s_tiles[0].max(axis=0, keepdims=True)
        for st in s_tiles[1:]:
            m = jnp.maximum(m, st.max(axis=0, keepdims=True))
        l = jnp.zeros((1, nc), F32)
        acc = jnp.zeros((dh, nc), F32)
        for st, vt in zip(s_tiles, v_tiles):
            p = jnp.exp(st - m)
            l = l + jnp.sum(p, axis=0, keepdims=True)
            acc = acc + _dot_tn(vt, p)
        o_s = acc / l
        wc = win_ref[0]
        wn = wn_ref[0]
        s_w = jnp.where(ok_wc, _dot(wc[:, g * dh:(g + 1) * dh], q_t), NEG_BIG)
        s_n = jnp.where(ok_new, _dot(wn[:, g * dh:(g + 1) * dh], q_t), NEG_BIG)
        m = jnp.maximum(s_w.max(axis=0, keepdims=True), s_n.max(axis=0, keepdims=True))
        p_w = jnp.exp(s_w - m)
        p_n = jnp.exp(s_n - m)
        l = jnp.sum(p_w, axis=0, keepdims=True) + jnp.sum(p_n, axis=0, keepdims=True)
        o_w = (_dot_tn(wc[:, NSA_KV_DIM + g * dh:NSA_KV_DIM + (g + 1) * dh], p_w)
               + _dot_tn(wn[:, NSA_KV_DIM + g * dh:NSA_KV_DIM + (g + 1) * dh], p_n)) / l
        gr = jax.nn.sigmoid(gate_ref[0, g])
        o_ref[0, g] = gr[0:1] * o_c + gr[1:2] * o_s + gr[2:3] * o_w


def nsa_sample_attend(h, q_t, gate_t, cmp, pool, win_cache, page_table):
    b, ts, _ = h.shape
    n_pages = page_table.shape[1]
    win_len = win_cache.shape[1]
    half = 2 * NSA_KV_DIM
    nc = NSA_HPG * ts
    page = lambda p: pl.BlockSpec((1, PAGE_SIZE, half), lambda i, pt: (pt[i, p], 0, 1))
    per_seq = lambda shape: pl.BlockSpec((1,) + shape, lambda i, pt: (i,) + (0,) * len(shape))
    return pl.pallas_call(
        functools.partial(_nsa_sample_kernel, n_pages=n_pages, ts=ts, win_len=win_len),
        out_shape=jax.ShapeDtypeStruct((b, NSA_KV_GROUPS, NSA_DH, nc), F32),
        grid_spec=pltpu.PrefetchScalarGridSpec(
            num_scalar_prefetch=1, grid=(b,),
            in_specs=[page(p) for p in range(n_pages)] + [
                per_seq((NSA_KV_GROUPS, NSA_DH, nc)), per_seq((NSA_KV_GROUPS, 8, nc)),
                pl.BlockSpec((1, ts, 4 * NSA_KV_DIM), lambda i, pt: (i, 0, NSA_QD // (4 * NSA_KV_DIM))),
                pl.BlockSpec((1, ts, half), lambda i, pt: (i, 0, (NSA_QD + 4 * NSA_KV_DIM) // half)),
                per_seq((NSA_KV_GROUPS, 2, cmp.shape[3], NSA_DH)), per_seq((win_len, half))],
            out_specs=per_seq((NSA_KV_GROUPS, NSA_DH, nc))),
        compiler_params=_compiler_params(1),
        name="nsa_sample_attend",
    )(page_table, *([pool] * n_pages), q_t, gate_t, h, h, cmp, win_cache)


def nsa_sample(h, kv_pool, win_cache, page_table, pe_k, pe_v, w1k, w2k, w1v, w2v):
    b, ts, _ = h.shape
    g, dh, nh = NSA_KV_GROUPS, NSA_DH, NSA_HPG
    assert NSA_QD % (4 * NSA_KV_DIM) == 0 and (NSA_QD + 4 * NSA_KV_DIM) % (2 * NSA_KV_DIM) == 0
    kv_new = h[:, :, NSA_QD:NSA_QD + 4 * NSA_KV_DIM].reshape(b, ts, 4, g, dh)
    win_new = h[:, :, NSA_QD + 4 * NSA_KV_DIM:NSA_QD + 6 * NSA_KV_DIM].reshape(b, ts, 2, g, dh)
    q_t = jnp.transpose(h[:, :, :NSA_QD].reshape(b, ts, g, nh, dh), (0, 2, 4, 3, 1)).reshape(b, g, dh, nh * ts)
    gate = h[:, :, NSA_QD + 6 * NSA_KV_DIM:NSA_IN].reshape(b, ts, g, nh, 3)
    gate_t = jnp.pad(jnp.transpose(gate, (0, 2, 4, 3, 1)).reshape(b, g, 3, nh * ts), ((0, 0), (0, 0), (0, 5), (0, 0)))
    pool = kv_pool.reshape(kv_pool.shape[0], PAGE_SIZE, 4 * NSA_KV_DIM)
    wc = win_cache.reshape(b, win_cache.shape[1], 2 * NSA_KV_DIM)
    pe = jnp.stack([pe_k, pe_v])
    w1 = jnp.stack([w1k, w1v]).reshape(2, CMP_LEN, dh, CMP_HID).astype(BF16)
    w2 = jnp.stack([w2k, w2v]).astype(BF16)
    cmp = nsa_compress_paged(pool, page_table, pe, w1, w2)
    o_t = nsa_sample_attend(h, q_t, gate_t, cmp, pool, wc, page_table)
    o = jnp.transpose(o_t.reshape(b, g, dh, nh, ts), (0, 4, 1, 3, 2)).reshape(b, ts, NSA_QD)
    win_all = jnp.concatenate([win_cache, win_new], axis=1)
    return o, kv_new, win_all[:, ts:]


def _pad_cols(w, n):
    return jnp.pad(w, ((0, 0), (0, n - w.shape[1])))


def kernel(x_prompt, x_sample, cache_nsa_kv, cache_nsa_win, state_gdn, state_gdn_conv, page_table,
           norm1_w, norm2_w, final_norm_w, gdn_w_in, gdn_conv_w, gdn_A_log, gdn_dt_bias, gdn_norm_w, gdn_w_out,
           nsa_w_in, nsa_pe_k, nsa_pe_v, nsa_cmp_w1_k, nsa_cmp_w2_k, nsa_cmp_w1_v, nsa_cmp_w2_v, nsa_w_out,
           ffn_w_in, ffn_w_out):
    bp, tp, d = x_prompt.shape
    bs, ts, _ = x_sample.shape
    mp, ms = bp * tp, bs * ts
    win_buf_len = cache_nsa_win.shape[2]
    x = jnp.concatenate([x_prompt.reshape(mp, d), x_sample.reshape(ms, d)], axis=0)
    gdn_n = _round_up(GDN_IN, LANES)
    nsa_n = _round_up(NSA_IN, LANES)
    kv_p, kv_s, win_p, win_s = [], [], [], []
    gs_p, gs_s, gc_p, gc_s = [], [], [], []
    for i in range(DEPTH):
        li = i // N_MIXERS
        if i % N_MIXERS == 0:
            h = norm_proj(x, norm1_w[i], _pad_cols(gdn_w_in[li], gdn_n).astype(BF16))
            hp = h[:mp].reshape(bp, tp, gdn_n)
            hs = h[mp:].reshape(bs, ts, gdn_n)
            gw = (gdn_conv_w[li], gdn_A_log[li], gdn_dt_bias[li], gdn_norm_w[li])
            s0 = jnp.zeros((bp, GDN_HEADS, GDN_DK, GDN_DV), F32)
            c0 = jnp.zeros((bp, GDN_CONV - 1, GDN_QKV), F32)
            op, sp, cp = gdn_mixer(hp, s0, c0, min(GDN_CHUNK, tp), *gw)
            os_, ss, cs = gdn_mixer(hs, state_gdn[li], state_gdn_conv[li], ts, *gw)
            gs_p.append(sp)
            gs_s.append(ss)
            gc_p.append(cp)
            gc_s.append(cs)
            o = jnp.concatenate([op.reshape(mp, -1), os_.reshape(ms, -1)], axis=0)
            x = proj_residual(x, o, gdn_w_out[li].astype(BF16))
        else:
            h = norm_proj(x, norm1_w[i], _pad_cols(nsa_w_in[li], nsa_n).astype(BF16))
            hp = h[:mp].reshape(bp, tp, nsa_n)
            hs = h[mp:].reshape(bs, ts, nsa_n)
            nw = (nsa_pe_k[li], nsa_pe_v[li], nsa_cmp_w1_k[li], nsa_cmp_w2_k[li], nsa_cmp_w1_v[li], nsa_cmp_w2_v[li])
            op, kp, wp = nsa_prompt(hp, win_buf_len, *nw)
            os_, ksm, wsm = nsa_sample(hs, cache_nsa_kv[li], cache_nsa_win[li], page_table, *nw)
            kv_p.append(kp)
            kv_s.append(ksm)
            win_p.append(wp)
            win_s.append(wsm)
            o = jnp.concatenate([op.reshape(mp, -1), os_.reshape(ms, -1)], axis=0)
            x = proj_residual(x, o, nsa_w_out[li].astype(BF16))
        x = ffn(x, norm2_w[i], ffn_w_in[i].astype(BF16), ffn_w_out[i].astype(BF16), final_norm_w, i == DEPTH - 1)
    y_prompt = x[:mp].reshape(bp, tp, d)
    y_sample = x[mp:].reshape(bs, ts, d)
    return (y_prompt, y_sample, jnp.stack(kv_p), jnp.stack(kv_s), jnp.stack(win_p), jnp.stack(win_s),
            jnp.stack(gs_p), jnp.stack(gs_s), jnp.stack(gc_p), jnp.stack(gc_s))
```

```python
import functools

import jax
import jax.numpy as jnp
from jax import lax
from jax.experimental import pallas as pl
from jax.experimental.pallas import tpu as pltpu

D_MODEL = 1024
DEPTH = 4
PAGE_SIZE = 128
N_MIXERS = 2
GDN_HEADS = 8
GDN_DK = 128
GDN_DV = 128
GDN_CONV = 4
GDN_CHUNK = 64
GDN_QKV = GDN_HEADS * (2 * GDN_DK + GDN_DV)
GDN_IN = GDN_QKV + GDN_HEADS * GDN_DV + 2 * GDN_HEADS
NSA_HEADS = 16
NSA_KV_GROUPS = 2
NSA_HPG = NSA_HEADS // NSA_KV_GROUPS
NSA_DH = 64
NSA_KV_DIM = NSA_KV_GROUPS * NSA_DH
CMP_LEN = 32
CMP_STRIDE = 16
CMP_HID = 256
SEL_BLOCK = 64
SEL_TOP = 16
WINDOW = 512
Q_BLOCK = 128
NSA_IN = NSA_HEADS * NSA_DH + 6 * NSA_KV_DIM + 3 * NSA_HEADS
D_FF = ((-(-8 * D_MODEL // 3) + 255) // 256) * 256
RMS_EPS = 1e-6
L2_EPS = 1e-6
NEG_BIG = -1e30

LANES = 128
VMEM_LIMIT = 56 * 1024 * 1024
ROW_TILE = 512
FF_CHUNK = 256

BF16 = jnp.bfloat16
F32 = jnp.float32


def _round_up(n, m):
    return -(-n // m) * m


def _rms_rows(x, w):
    return x * lax.rsqrt(jnp.mean(x * x, axis=-1, keepdims=True) + RMS_EPS) * w


def _compiler_params(n_axes):
    return pltpu.CompilerParams(dimension_semantics=("arbitrary",) * n_axes, vmem_limit_bytes=VMEM_LIMIT)


def _resident(shape):
    return pl.BlockSpec(shape, lambda *_: (0,) * len(shape), pipeline_mode=pl.Buffered(1))


def _norm_proj_kernel(x_ref, nw_ref, w_ref, o_ref):
    xn = _rms_rows(x_ref[...], nw_ref[...]).astype(BF16)
    n = w_ref.shape[1]
    for c0 in range(0, n, 512):
        c1 = min(c0 + 512, n)
        o_ref[:, c0:c1] = jnp.dot(xn, w_ref[:, c0:c1], preferred_element_type=F32)


def norm_proj(x, nw, w):
    m, d = x.shape
    n = w.shape[1]
    tm = 256
    return pl.pallas_call(
        _norm_proj_kernel,
        out_shape=jax.ShapeDtypeStruct((m, n), F32),
        grid=(m // tm,),
        in_specs=[pl.BlockSpec((tm, d), lambda i: (i, 0)), _resident((1, d)), _resident((d, n))],
        out_specs=pl.BlockSpec((tm, n), lambda i: (i, 0)),
        compiler_params=_compiler_params(1),
        name="norm_proj",
    )(x, nw.reshape(1, d), w)


def _proj_residual_kernel(x_ref, a_ref, w_ref, o_ref):
    o_ref[...] = x_ref[...] + jnp.dot(a_ref[...].astype(BF16), w_ref[...], preferred_element_type=F32)


def proj_residual(x, a, w):
    m, d = x.shape
    k = a.shape[1]
    tm = ROW_TILE
    return pl.pallas_call(
        _proj_residual_kernel,
        out_shape=jax.ShapeDtypeStruct((m, d), F32),
        grid=(m // tm,),
        in_specs=[pl.BlockSpec((tm, d), lambda i: (i, 0)), pl.BlockSpec((tm, k), lambda i: (i, 0)), _resident((k, d))],
        out_specs=pl.BlockSpec((tm, d), lambda i: (i, 0)),
        compiler_params=_compiler_params(1),
        name="proj_residual",
    )(x, a, w)


def _ffn_kernel(x_ref, nw_ref, wi_ref, wo_ref, fw_ref, o_ref, act_ref, *, final_norm):
    x = x_ref[...]
    xn = _rms_rows(x, nw_ref[...]).astype(BF16)
    for c0 in range(0, D_FF, FF_CHUNK):
        g = jnp.dot(xn, wi_ref[:, c0:c0 + FF_CHUNK], preferred_element_type=F32)
        u = jnp.dot(xn, wi_ref[:, D_FF + c0:D_FF + c0 + FF_CHUNK], preferred_element_type=F32)
        act_ref[:, c0:c0 + FF_CHUNK] = (g * jax.nn.sigmoid(g) * u).astype(BF16)
    y = x + jnp.dot(act_ref[...], wo_ref[...], preferred_element_type=F32)
    if final_norm:
        y = _rms_rows(y, fw_ref[...])
    o_ref[...] = y


def ffn(x, nw, w_in, w_out, final_w, final_norm):
    m, d = x.shape
    tm = ROW_TILE
    return pl.pallas_call(
        functools.partial(_ffn_kernel, final_norm=final_norm),
        out_shape=jax.ShapeDtypeStruct((m, d), F32),
        grid=(m // tm,),
        in_specs=[pl.BlockSpec((tm, d), lambda i: (i, 0)), _resident((1, d)), _resident((d, 2 * D_FF)),
                  _resident((D_FF, d)), _resident((1, d))],
        out_specs=pl.BlockSpec((tm, d), lambda i: (i, 0)),
        scratch_shapes=[pltpu.VMEM((tm, D_FF), BF16)],
        compiler_params=_compiler_params(1),
        name="ffn",
    )(x, nw.reshape(1, d), w_in, w_out, final_w.reshape(1, d))


GDN_N = _round_up(GDN_IN, LANES)
CONV_HALO = 8
GDN_SEQS_PER_STEP = 2


def _gdn_prep_kernel(h_ref, halo_ref, cbuf_ref, cw_ref, alog_ref, dt_ref, qkv_ref, bg_ref, xbuf_ref, *, tt):
    t = pl.program_id(1)
    prev = jnp.where(t == 0, cbuf_ref[0], halo_ref[0])
    xbuf_ref[0:CONV_HALO, :] = prev
    xbuf_ref[CONV_HALO:CONV_HALO + tt, :] = h_ref[0, :, 0:GDN_QKV]
    for c in range(GDN_QKV // LANES):
        cs = slice(c * LANES, (c + 1) * LANES)
        y = xbuf_ref[CONV_HALO:CONV_HALO + tt, cs] * cw_ref[GDN_CONV - 1:GDN_CONV, cs]
        for i in range(GDN_CONV - 1):
            r0 = CONV_HALO - (GDN_CONV - 1) + i
            y = y + xbuf_ref[r0:r0 + tt, cs] * cw_ref[i:i + 1, cs]
        y = y * jax.nn.sigmoid(y)
        if c < 2 * GDN_HEADS:
            scale = GDN_DK ** -0.5 if c < GDN_HEADS else 1.0
            y = y * (lax.rsqrt(jnp.sum(y * y, axis=-1, keepdims=True) + L2_EPS) * scale)
        qkv_ref[0, :, cs] = y
    ba = h_ref[0, :, GDN_QKV + GDN_HEADS * GDN_DV:GDN_N]
    lane = lax.broadcasted_iota(jnp.int32, ba.shape, 1)
    xs = ba + dt_ref[...]
    softplus = jnp.maximum(xs, 0.0) + jnp.log(1.0 + jnp.exp(-jnp.abs(xs)))
    g = -jnp.exp(alog_ref[...]) * softplus
    bg_ref[0] = jnp.where(lane < GDN_HEADS, jax.nn.sigmoid(ba), g)


def gdn_prep(h, conv_buf, conv_w, a_log, dt_bias):
    b, t, _ = h.shape
    tt = min(t, 256)
    cbuf = jnp.pad(conv_buf, ((0, 0), (CONV_HALO - (GDN_CONV - 1), 0), (0, 0)))
    pad8 = lambda v: jnp.pad(v, (GDN_HEADS, LANES - 2 * GDN_HEADS)).reshape(1, LANES)
    hb = tt // CONV_HALO
    return pl.pallas_call(
        functools.partial(_gdn_prep_kernel, tt=tt),
        out_shape=(jax.ShapeDtypeStruct((b, t, GDN_QKV), F32), jax.ShapeDtypeStruct((b, t, LANES), F32)),
        grid=(b, t // tt),
        in_specs=[pl.BlockSpec((1, tt, GDN_N), lambda i, j: (i, j, 0)),
                  pl.BlockSpec((1, CONV_HALO, GDN_QKV), lambda i, j: (i, jnp.maximum(j * hb - 1, 0), 0)),
                  pl.BlockSpec((1, CONV_HALO, GDN_QKV), lambda i, j: (i, 0, 0)),
                  _resident((GDN_CONV, GDN_QKV)), _resident((1, LANES)), _resident((1, LANES))],
        out_specs=(pl.BlockSpec((1, tt, GDN_QKV), lambda i, j: (i, j, 0)),
                   pl.BlockSpec((1, tt, LANES), lambda i, j: (i, j, 0))),
        scratch_shapes=[pltpu.VMEM((CONV_HALO + tt, GDN_QKV), F32)],
        compiler_params=_compiler_params(2),
        name="gdn_prep",
    )(h, h, cbuf, conv_w, pad8(a_log), pad8(dt_bias))


def _dot(a, b):
    return jnp.dot(a.astype(BF16), b.astype(BF16), preferred_element_type=F32)


def _dot_nt(a, b):
    return lax.dot_general(a.astype(BF16), b.astype(BF16), (((1,), (1,)), ((), ())), preferred_element_type=F32)


def _dot_tn(a, b):
    return lax.dot_general(a.astype(BF16), b.astype(BF16), (((0,), (0,)), ((), ())), preferred_element_type=F32)


def _dot_f32(a, b):
    return jnp.dot(a, b, precision=lax.Precision.HIGHEST, preferred_element_type=F32)


def _gdn_scan_kernel(q_ref, k_ref, v_ref, bg_ref, z_ref, s0_ref, nw_ref, o_ref, s_ref, *, c, nb):
    @pl.when(pl.program_id(1) == 0)
    def _():
        s_ref[...] = s0_ref[...]

    row = lax.broadcasted_iota(jnp.int32, (c, c), 0)
    col = lax.broadcasted_iota(jnp.int32, (c, c), 1)
    incl = row >= col
    strict = row > col
    tri = incl.astype(F32)
    nw = nw_ref[...]
    n_double = c.bit_length() - 2
    chains = [(b, h) for b in range(nb) for h in range(GDN_HEADS)]
    cols = lambda h: slice(h * GDN_DK, (h + 1) * GDN_DK)
    each = lambda f, *lists: [f(*a) for a in zip(*lists)]
    q = [q_ref[b, :, cols(h)] for b, h in chains]
    k = [k_ref[b, :, cols(h)] for b, h in chains]
    v = [v_ref[b, :, cols(h)] for b, h in chains]
    beta = [bg_ref[b, :, h:h + 1] for b, h in chains]
    g = [bg_ref[b, :, GDN_HEADS + h:GDN_HEADS + h + 1] for b, h in chains]
    diff = each(lambda g_: _dot_f32(tri, jnp.where(strict, g_, 0.0)), g)
    gb = each(lambda g_: _dot_f32(tri, jnp.broadcast_to(g_, (c, GDN_DK))), g)
    kk = each(_dot_nt, k, k)
    qk = each(_dot_nt, q, k)
    decay = each(lambda d: jnp.where(incl, jnp.exp(d), 0.0), diff)
    eg = each(jnp.exp, gb)
    g_last = each(lambda gb_: gb_[c - 1:c, :], gb)
    tail = each(lambda gl, gb_: jnp.exp(gl - gb_), g_last, gb)
    p = each(lambda b_, kk_, d: jnp.where(strict, b_ * kk_ * d, 0.0), beta, kk, decay)
    x = each(lambda l_: -l_, p)
    for _ in range(n_double):
        p = each(_dot, p, p)
        x = each(lambda x_, p_: x_ + p_ + _dot(x_, p_), x, p)
    r = each(lambda b_, v_, e, k_: jnp.concatenate([b_ * v_, (b_ * e) * k_], axis=1), beta, v, eg, k)
    tr = each(lambda r_, x_: r_ + _dot(x_, r_), r, x)
    s = [s_ref[b, h] for b, h in chains]
    ws = each(lambda t, s_: _dot(t[:, GDN_DV:], s_), tr, s)
    qs = each(_dot, q, s)
    u = each(lambda t, ws_: t[:, :GDN_DV] - ws_, tr, ws)
    o = each(lambda e, qs_, qk_, d, u_: e * qs_ + _dot(qk_ * d, u_), eg, qs, qk, decay, u)
    s_new = each(lambda gl, s_, k_, t, u_: jnp.exp(gl) * s_ + _dot_tn(k_ * t, u_), g_last, s, k, tail, u)
    for (b, h), s_, o_ in zip(chains, s_new, o):
        s_ref[b, h] = s_
        on = o_ * lax.rsqrt(jnp.mean(o_ * o_, axis=-1, keepdims=True) + RMS_EPS) * nw
        z = z_ref[b, :, cols(h)]
        o_ref[b, :, cols(h)] = on * (z * jax.nn.sigmoid(z))


def gdn_scan(qkv, bg, h, s0, norm_w, chunk):
    b, t, _ = qkv.shape
    assert chunk >= 8 and chunk & (chunk - 1) == 0 and t % chunk == 0
    nb = GDN_SEQS_PER_STEP if b % GDN_SEQS_PER_STEP == 0 else 1
    hd = GDN_HEADS * GDN_DK
    blk = lambda col: pl.BlockSpec((nb, chunk, hd), lambda i, j: (i, j, col))
    return pl.pallas_call(
        functools.partial(_gdn_scan_kernel, c=chunk, nb=nb),
        out_shape=(jax.ShapeDtypeStruct((b, t, GDN_HEADS * GDN_DV), F32),
                   jax.ShapeDtypeStruct((b, GDN_HEADS, GDN_DK, GDN_DV), F32)),
        grid=(b // nb, t // chunk),
        in_specs=[blk(0), blk(1), blk(2), pl.BlockSpec((nb, chunk, LANES), lambda i, j: (i, j, 0)), blk(3),
                  pl.BlockSpec((nb, GDN_HEADS, GDN_DK, GDN_DV), lambda i, j: (i, 0, 0, 0)), _resident((1, GDN_DV))],
        out_specs=(pl.BlockSpec((nb, chunk, GDN_HEADS * GDN_DV), lambda i, j: (i, j, 0)),
                   pl.BlockSpec((nb, GDN_HEADS, GDN_DK, GDN_DV), lambda i, j: (i, 0, 0, 0))),
        compiler_params=_compiler_params(2),
        name="gdn_scan",
    )(qkv, qkv, qkv, bg, h, s0, norm_w.reshape(1, GDN_DV))


def gdn_mixer(h, s0, conv_buf, chunk, conv_w, a_log, dt_bias, norm_w):
    t = h.shape[1]
    qkv, bg = gdn_prep(h, conv_buf, conv_w, a_log, dt_bias)
    o, s = gdn_scan(qkv, bg, h, s0, norm_w, chunk)
    new_buf = jnp.concatenate([conv_buf, h[:, max(t - (GDN_CONV - 1), 0):, :GDN_QKV]], axis=1)[:, -(GDN_CONV - 1):]
    return o, s, new_buf


NSA_N = _round_up(NSA_IN, LANES)
NSA_QD = NSA_HEADS * NSA_DH
SEL_TILE = 256
assert CMP_LEN == 2 * CMP_STRIDE


def _nsa_compress_kernel(x_ref, pe_ref, w1_ref, w2_ref, o_ref, *, nblk):
    acc_a = jnp.zeros((nblk, CMP_HID), F32)
    acc_b = jnp.zeros((nblk, CMP_HID), F32)
    for l in range(CMP_STRIDE):
        xl = x_ref[0, 0, 0, pl.ds(l, nblk, stride=CMP_STRIDE), :]
        acc_a = acc_a + _dot(xl + pe_ref[0, l:l + 1, :], w1_ref[0, l])
        acc_b = acc_b + _dot(xl + pe_ref[0, CMP_STRIDE + l:CMP_STRIDE + l + 1, :], w1_ref[0, CMP_STRIDE + l])
    pre = acc_a + pltpu.roll(acc_b, nblk - 1, 0)
    hid = pre * jax.nn.sigmoid(pre)
    out = _dot(hid, w2_ref[0])
    row = lax.broadcasted_iota(jnp.int32, out.shape, 0)
    o_ref[0, 0, 0] = jnp.where(row < nblk - 1, out, 0.0)


def nsa_compress(x, pe, w1, w2):
    b, g, _, t, dh = x.shape
    nblk = t // CMP_STRIDE
    return pl.pallas_call(
        functools.partial(_nsa_compress_kernel, nblk=nblk),
        out_shape=jax.ShapeDtypeStruct((b, g, 2, nblk, dh), F32),
        grid=(b, g, 2),
        in_specs=[pl.BlockSpec((1, 1, 1, t, dh), lambda i, j, k: (i, j, k, 0, 0)),
                  pl.BlockSpec((1, CMP_LEN, dh), lambda i, j, k: (k, 0, 0)),
                  pl.BlockSpec((1, CMP_LEN, dh, CMP_HID), lambda i, j, k: (k, 0, 0, 0)),
                  pl.BlockSpec((1, CMP_HID, dh), lambda i, j, k: (k, 0, 0))],
        out_specs=pl.BlockSpec((1, 1, 1, nblk, dh), lambda i, j, k: (i, j, k, 0, 0)),
        compiler_params=_compiler_params(3),
        name="nsa_compress",
    )(x, pe, w1, w2)


WIN_TILE = 128
V_ROWS = NSA_DH + 16


def _heads_on_lanes(x, n_heads):
    return jnp.concatenate([x] * n_heads, axis=1)


def _select_blocks(psum_t, q_pos, n_sel):
    ncp, tq = psum_t.shape
    n_rows = _round_up(n_sel, 8)
    j_i = lax.broadcasted_iota(jnp.int32, (LANES, ncp), 0)
    n_i = lax.broadcasted_iota(jnp.int32, (LANES, ncp), 1)
    overlap_t = ((n_i >= 4 * j_i - 1) & (n_i <= 4 * j_i + 3) & (j_i < n_sel)).astype(F32)
    imp = _dot_f32(overlap_t, psum_t)[:n_rows]
    jrow = lax.broadcasted_iota(jnp.int32, (n_rows, tq), 0)
    cur = jnp.right_shift(q_pos, 6)
    forced = (jrow == 0) | (jrow == cur) | (jrow == cur - 1)
    st = jnp.where(jrow > cur, -jnp.inf, jnp.where(forced, jnp.inf, imp))
    cnt = jnp.zeros(st.shape, F32)
    for jp in range(n_sel):
        r = st[jp:jp + 1, :]
        ahead = (r > st) | ((r == st) & (jrow > jp))
        cnt = cnt + jnp.where(ahead, 1.0, 0.0)
    return jnp.where(cnt < SEL_TOP, 1.0, 0.0).astype(BF16)


def _nsa_prompt_kernel(q_ref, gate_ref, cmp_ref, ks_ref, vs_ref, kw_ref, vw_ref, o_ref, *, n_sel):
    i = pl.program_id(2)
    tq, nh, dh = Q_BLOCK, NSA_HPG, NSA_DH
    ncp = cmp_ref.shape[3]
    q_pos = i * tq + lax.broadcasted_iota(jnp.int32, (1, tq), 1)
    q_t = jnp.concatenate([(q_ref[0, :, h * dh:(h + 1) * dh] * dh ** -0.5).T for h in range(nh)], axis=1).astype(BF16)

    kc = cmp_ref[0, 0, 0].astype(BF16)
    vc_t = cmp_ref[0, 0, 1].T.astype(BF16)
    n_idx = lax.broadcasted_iota(jnp.int32, (ncp, tq), 0)
    ok_c = _heads_on_lanes(n_idx * CMP_STRIDE + (CMP_LEN - 1) <= q_pos, nh)
    s = jnp.where(ok_c, _dot(kc, q_t), NEG_BIG)
    e = jnp.where(ok_c, jnp.exp(s - jnp.max(s, axis=0, keepdims=True)), 0.0)
    p_c = e / jnp.maximum(jnp.sum(e, axis=0, keepdims=True), 1e-30)
    o_c = _dot(vc_t, p_c)
    psum = p_c[:, 0:tq]
    for h in range(1, nh):
        psum = psum + p_c[:, h * tq:(h + 1) * tq]
    sel = _select_blocks(psum, q_pos, n_sel)

    n_tiles = (i * tq + tq + SEL_TILE - 1) // SEL_TILE

    def body(kt, carry):
        m, acc = carry
        cr = lax.broadcasted_iota(jnp.int32, (SEL_TILE, n_sel), 0)
        jc = lax.broadcasted_iota(jnp.int32, (SEL_TILE, n_sel), 1)
        expand = jnp.where(jc == kt * (SEL_TILE // SEL_BLOCK) + jnp.right_shift(cr, 6), 1.0, 0.0).astype(BF16)
        kpos = kt * SEL_TILE + lax.broadcasted_iota(jnp.int32, (SEL_TILE, tq), 0)
        ok = (_dot(expand, sel) > 0.5) & (kpos <= q_pos)
        bias = _heads_on_lanes(jnp.where(ok, 0.0, NEG_BIG), nh)
        k0 = pl.multiple_of(kt * SEL_TILE, SEL_TILE)
        s = _dot(ks_ref[0, 0, 0, pl.ds(k0, SEL_TILE), :], q_t) + bias
        m_new = jnp.maximum(m, jnp.max(s, axis=0, keepdims=True))
        p = jnp.exp(s - m_new)
        acc = jnp.exp(m - m_new) * acc + _dot(vs_ref[0, 0, 0, kt], p)
        return m_new, acc

    init = (jnp.full((1, nh * tq), NEG_BIG, F32), jnp.zeros((V_ROWS, nh * tq), F32))
    _, acc = lax.fori_loop(0, n_tiles, body, init)
    o_s = acc[0:dh] / acc[dh:dh + 1]

    wlen = WINDOW + tq
    ws = pl.multiple_of(jnp.maximum(i * tq - WINDOW, 0), WIN_TILE)
    wpos = ws + lax.broadcasted_iota(jnp.int32, (wlen, tq), 0)
    ok_w = _heads_on_lanes((wpos <= q_pos) & (wpos >= q_pos - WINDOW), nh)
    s = jnp.where(ok_w, _dot(kw_ref[0, 0, 0, pl.ds(ws, wlen), :], q_t), NEG_BIG)
    p_w = jnp.exp(s - jnp.max(s, axis=0, keepdims=True)).astype(BF16)
    wt0 = ws // WIN_TILE
    acc_w = jnp.zeros((V_ROWS, nh * tq), F32)
    for t in range(wlen // WIN_TILE):
        acc_w = acc_w + jnp.dot(vw_ref[0, 0, 0, wt0 + t], p_w[t * WIN_TILE:(t + 1) * WIN_TILE],
                                preferred_element_type=F32)
    o_w = acc_w[0:dh] / acc_w[dh:dh + 1]

    g_t = jax.nn.sigmoid(gate_ref[0, 0]).T
    gate_row = lambda br: jnp.concatenate([g_t[3 * h + br:3 * h + br + 1, :] for h in range(nh)], axis=1)
    o_t = gate_row(0) * o_c + gate_row(1) * o_s + gate_row(2) * o_w
    o_ref[0] = jnp.concatenate([o_t[:, h * tq:(h + 1) * tq].T for h in range(nh)], axis=1)


def nsa_prompt_attend(h, gate, cmp, k_rows, vs_t, vw_t):
    b, t, _ = h.shape
    g = NSA_KV_GROUPS
    n_sel = t // SEL_BLOCK
    assert t % SEL_TILE == 0 and t >= WINDOW + Q_BLOCK and n_sel <= LANES and n_sel % 8 == 0
    gw = NSA_HPG * NSA_DH
    kr = lambda kind: pl.BlockSpec((1, 1, 1, t, NSA_DH), lambda i, j, k: (i, j, kind, 0, 0))
    vt = lambda tile: pl.BlockSpec((1, 1, 1, t // tile, V_ROWS, tile), lambda i, j, k: (i, j, 0, 0, 0, 0))
    return pl.pallas_call(
        functools.partial(_nsa_prompt_kernel, n_sel=n_sel),
        out_shape=jax.ShapeDtypeStruct((b, t, NSA_QD), F32),
        grid=(b, g, t // Q_BLOCK),
        in_specs=[pl.BlockSpec((1, Q_BLOCK, gw), lambda i, j, k: (i, k, j)),
                  pl.BlockSpec((1, 1, Q_BLOCK, LANES), lambda i, j, k: (i, j, k, 0)),
                  pl.BlockSpec((1, 1, 2, t // CMP_STRIDE, NSA_DH), lambda i, j, k: (i, j, 0, 0, 0)),
                  kr(0), vt(SEL_TILE), kr(1), vt(WIN_TILE)],
        out_specs=pl.BlockSpec((1, Q_BLOCK, gw), lambda i, j, k: (i, k, j)),
        compiler_params=_compiler_params(3),
        name="nsa_prompt_attend",
    )(h, gate, cmp, k_rows, vs_t, k_rows, vw_t)


def _value_tiles_t(v, tile):
    b, g, t, dh = v.shape
    vt = jnp.transpose(v.reshape(b, g, t // tile, tile, dh), (0, 1, 2, 4, 3))
    extra = jnp.zeros((b, g, t // tile, V_ROWS - dh, tile), v.dtype).at[:, :, :, 0, :].set(1.0)
    return jnp.concatenate([vt, extra], axis=3).astype(BF16)[:, :, None]


def nsa_prompt(h, win_buf_len, pe_k, pe_v, w1k, w2k, w1v, w2v):
    b, t, _ = h.shape
    g, dh = NSA_KV_GROUPS, NSA_DH
    kv_rows = h[:, :, NSA_QD:NSA_QD + 4 * NSA_KV_DIM].reshape(b, t, 4, g, dh)
    win_rows = h[:, :, NSA_QD + 4 * NSA_KV_DIM:NSA_QD + 6 * NSA_KV_DIM].reshape(b, t, 2, g, dh)
    by_group = lambda x: jnp.transpose(x, (0, 3, 2, 1, 4))
    x_cmp = by_group(kv_rows[:, :, 0:2])
    k_rows = by_group(jnp.stack([kv_rows[:, :, 2], win_rows[:, :, 0]], axis=2)).astype(BF16)
    vs_t = _value_tiles_t(jnp.transpose(kv_rows[:, :, 3], (0, 2, 1, 3)), SEL_TILE)
    vw_t = _value_tiles_t(jnp.transpose(win_rows[:, :, 1], (0, 2, 1, 3)), WIN_TILE)
    gate = h[:, :, NSA_QD + 6 * NSA_KV_DIM:NSA_IN].reshape(b, t, g, 3 * NSA_HPG)
    gate = jnp.pad(jnp.transpose(gate, (0, 2, 1, 3)), ((0, 0), (0, 0), (0, 0), (0, LANES - 3 * NSA_HPG)))
    pe = jnp.stack([pe_k, pe_v])
    w1 = jnp.stack([w1k, w1v]).reshape(2, CMP_LEN, dh, CMP_HID).astype(BF16)
    w2 = jnp.stack([w2k, w2v]).astype(BF16)
    cmp = nsa_compress(x_cmp, pe, w1, w2)
    o = nsa_prompt_attend(h, gate, cmp, k_rows, vs_t, vw_t)
    wbuf = jnp.concatenate([jnp.zeros((b, win_buf_len, 2, g, dh), h.dtype), win_rows], axis=1)[:, t:]
    return o, kv_rows, wbuf


def _nsa_compress_paged_kernel(pt_ref, *refs, n_pages):
    del pt_ref
    page_refs = (refs[:n_pages], refs[n_pages:2 * n_pages])
    pe_ref, w1_ref, w2_ref, o_ref, rows_ref = refs[2 * n_pages:]
    dh = NSA_DH
    rpp = PAGE_SIZE // CMP_STRIDE
    nblk = n_pages * rpp
    row = lax.broadcasted_iota(jnp.int32, (nblk, dh), 0)
    for kind in range(2):
        for p, r in enumerate(page_refs[kind]):
            rows_ref[p * PAGE_SIZE:(p + 1) * PAGE_SIZE, :] = r[0, 0].T
        acc_a = [jnp.zeros((nblk, CMP_HID), F32) for _ in range(NSA_KV_GROUPS)]
        acc_b = [jnp.zeros((nblk, CMP_HID), F32) for _ in range(NSA_KV_GROUPS)]
        for l in range(CMP_STRIDE):
            xl = rows_ref[pl.ds(l, nblk, stride=CMP_STRIDE), :]
            for g in range(NSA_KV_GROUPS):
                x = xl[:, g * dh:(g + 1) * dh]
                acc_a[g] = acc_a[g] + _dot(x + pe_ref[kind, l:l + 1, :], w1_ref[kind, l])
                acc_b[g] = acc_b[g] + _dot(x + pe_ref[kind, CMP_STRIDE + l:CMP_STRIDE + l + 1, :],
                                           w1_ref[kind, CMP_STRIDE + l])
        for g in range(NSA_KV_GROUPS):
            pre = acc_a[g] + pltpu.roll(acc_b[g], nblk - 1, 0)
            hid = pre * jax.nn.sigmoid(pre)
            o_ref[0, g, kind] = jnp.where(row < nblk - 1, _dot(hid, w2_ref[kind]), 0.0)


def nsa_compress_paged(pool_t, layer, page_table, pe, w1, w2):
    b, n_pages = page_table.shape
    nblk = n_pages * (PAGE_SIZE // CMP_STRIDE)
    page = lambda kind, p: pl.BlockSpec((1, 1, NSA_KV_DIM, PAGE_SIZE), lambda i, pt: (layer, pt[i, p], kind, 0))
    const = lambda shape: pl.BlockSpec(shape, lambda i, pt: (0,) * len(shape))
    return pl.pallas_call(
        functools.partial(_nsa_compress_paged_kernel, n_pages=n_pages),
        out_shape=jax.ShapeDtypeStruct((b, NSA_KV_GROUPS, 2, nblk, NSA_DH), F32),
        grid_spec=pltpu.PrefetchScalarGridSpec(
            num_scalar_prefetch=1, grid=(b,),
            in_specs=[page(kind, p) for kind in range(2) for p in range(n_pages)]
            + [const(pe.shape), const(w1.shape), const(w2.shape)],
            out_specs=pl.BlockSpec((1, NSA_KV_GROUPS, 2, nblk, NSA_DH), lambda i, pt: (i, 0, 0, 0, 0)),
            scratch_shapes=[pltpu.VMEM((n_pages * PAGE_SIZE, NSA_KV_DIM), F32)]),
        compiler_params=_compiler_params(1),
        name="nsa_compress_paged",
    )(page_table, *([pool_t] * (2 * n_pages)), pe, w1, w2)


def _nsa_sample_kernel(pt_ref, *refs, n_pages, ts, win_len):
    del pt_ref
    page_refs = refs[:n_pages]
    q_ref, gate_ref, kvn_ref, wn_ref, cmp_ref, win_ref, o_ref = refs[n_pages:]
    dh, nh, g_n = NSA_DH, NSA_HPG, NSA_KV_GROUPS
    nc = nh * ts
    past = n_pages * PAGE_SIZE
    n_sel = -(-(past + ts) // SEL_BLOCK)
    ncp = cmp_ref.shape[3]
    tcol = lax.broadcasted_iota(jnp.int32, (1, nc), 1) % ts
    q_pos = past + tcol
    cc = lax.broadcasted_iota(jnp.int32, (nc, LANES), 0) % ts
    tl = lax.broadcasted_iota(jnp.int32, (nc, LANES), 1)
    head_sum = (cc == tl).astype(F32)
    tr = lax.broadcasted_iota(jnp.int32, (LANES, nc), 0)
    ce = lax.broadcasted_iota(jnp.int32, (LANES, nc), 1) % ts
    head_expand = (tr == ce).astype(BF16)
    q_pos_l = past + jnp.minimum(lax.broadcasted_iota(jnp.int32, (1, LANES), 1), ts - 1)
    n_idx = lax.broadcasted_iota(jnp.int32, (ncp, nc), 0)
    ok_c = n_idx * CMP_STRIDE + (CMP_LEN - 1) <= q_pos
    key_in_page = lax.broadcasted_iota(jnp.int32, (PAGE_SIZE, nc), 0)
    t_new = lax.broadcasted_iota(jnp.int32, (ts, nc), 0)
    ok_new = t_new <= tcol
    widx = lax.broadcasted_iota(jnp.int32, (win_len, nc), 0)
    wpos = past - win_len + widx
    ok_wc = (wpos <= q_pos) & (wpos >= q_pos - WINDOW) & (wpos >= 0)

    for g in range(g_n):
        q_t = (q_ref[0, g] * dh ** -0.5).astype(BF16)
        kc = cmp_ref[0, g, 0].astype(BF16)
        vc = cmp_ref[0, g, 1]
        s = jnp.where(ok_c, _dot(kc, q_t), NEG_BIG)
        e = jnp.where(ok_c, jnp.exp(s - jnp.max(s, axis=0, keepdims=True)), 0.0)
        p_c = e / jnp.maximum(jnp.sum(e, axis=0, keepdims=True), 1e-30)
        o_c = _dot_tn(vc, p_c)
        sel = _select_blocks(_dot_f32(p_c, head_sum), q_pos_l, n_sel)
        sel_cols = _dot(sel, head_expand)
        s_tiles, v_tiles = [], []
        for p in range(n_pages):
            blk = page_refs[p][0, 0]
            k_t = blk[g * dh:(g + 1) * dh, :]
            v_tiles.append(blk[NSA_KV_DIM + g * dh:NSA_KV_DIM + (g + 1) * dh, :])
            per_page = PAGE_SIZE // SEL_BLOCK
            okp = sel_cols[per_page * p:per_page * p + 1, :]
            for jb in range(1, per_page):
                okp = jnp.where(key_in_page < jb * SEL_BLOCK, okp, sel_cols[per_page * p + jb:per_page * p + jb + 1, :])
            s_tiles.append(jnp.where(okp > 0.5, _dot_tn(k_t, q_t), NEG_BIG))
        kvn = kvn_ref[0]
        k_new = kvn[:, 2 * NSA_KV_DIM + g * dh:2 * NSA_KV_DIM + (g + 1) * dh]
        v_new = kvn[:, 3 * NSA_KV_DIM + g * dh:3 * NSA_KV_DIM + (g + 1) * dh]
        s_new = jnp.where(ok_new, _dot(k_new, q_t), NEG_BIG)
        m = s_new.max(axis=0, keepdims=True)
        for st in s_tiles:
            m = jnp.maximum(m, st.max(axis=0, keepdims=True))
        p = jnp.exp(s_new - m)
        l = jnp.sum(p, axis=0, keepdims=True)
        acc = _dot_tn(v_new, p)
        for st, vt in zip(s_tiles, v_tiles):
            p = jnp.exp(st - m)
            l = l + jnp.sum(p, axis=0, keepdims=True)
            acc = acc + _dot(vt, p)
        o_s = acc / l
        wc = win_ref[0, 0]
        wn = wn_ref[0]
        s_w = jnp.where(ok_wc, _dot_tn(wc[g * dh:(g + 1) * dh, :], q_t), NEG_BIG)
        s_n = jnp.where(ok_new, _dot(wn[:, g * dh:(g + 1) * dh], q_t), NEG_BIG)
        m = jnp.maximum(s_w.max(axis=0, keepdims=True), s_n.max(axis=0, keepdims=True))
        p_w = jnp.exp(s_w - m)
        p_n = jnp.exp(s_n - m)
        l = jnp.sum(p_w, axis=0, keepdims=True) + jnp.sum(p_n, axis=0, keepdims=True)
        o_w = (_dot(wc[NSA_KV_DIM + g * dh:NSA_KV_DIM + (g + 1) * dh, :], p_w)
               + _dot_tn(wn[:, NSA_KV_DIM + g * dh:NSA_KV_DIM + (g + 1) * dh], p_n)) / l
        gr = jax.nn.sigmoid(gate_ref[0, g])
        o_ref[0, g] = gr[0:1] * o_c + gr[1:2] * o_s + gr[2:3] * o_w


def nsa_sample_attend(h, q_t, gate_t, cmp, pool_t, win_t, layer, page_table):
    b, ts, _ = h.shape
    n_pages = page_table.shape[1]
    win_len = win_t.shape[3]
    half = 2 * NSA_KV_DIM
    nc = NSA_HPG * ts
    page = lambda p: pl.BlockSpec((1, 1, half, PAGE_SIZE), lambda i, pt: (layer, pt[i, p], 1, 0))
    per_seq = lambda shape: pl.BlockSpec((1,) + shape, lambda i, pt: (i,) + (0,) * len(shape))
    return pl.pallas_call(
        functools.partial(_nsa_sample_kernel, n_pages=n_pages, ts=ts, win_len=win_len),
        out_shape=jax.ShapeDtypeStruct((b, NSA_KV_GROUPS, NSA_DH, nc), F32),
        grid_spec=pltpu.PrefetchScalarGridSpec(
            num_scalar_prefetch=1, grid=(b,),
            in_specs=[page(p) for p in range(n_pages)] + [
                per_seq((NSA_KV_GROUPS, NSA_DH, nc)), per_seq((NSA_KV_GROUPS, 8, nc)),
                pl.BlockSpec((1, ts, 4 * NSA_KV_DIM), lambda i, pt: (i, 0, NSA_QD // (4 * NSA_KV_DIM))),
                pl.BlockSpec((1, ts, half), lambda i, pt: (i, 0, (NSA_QD + 4 * NSA_KV_DIM) // half)),
                per_seq((NSA_KV_GROUPS, 2, cmp.shape[3], NSA_DH)),
                pl.BlockSpec((1, 1, half, win_len), lambda i, pt: (layer, i, 0, 0))],
            out_specs=per_seq((NSA_KV_GROUPS, NSA_DH, nc))),
        compiler_params=_compiler_params(1),
        name="nsa_sample_attend",
    )(page_table, *([pool_t] * n_pages), q_t, gate_t, h, h, cmp, win_t)


def _feature_major(cache):
    nd = cache.ndim
    t = jnp.transpose(cache, tuple(range(nd - 4)) + (nd - 3, nd - 2, nd - 1, nd - 4))
    return t.reshape(t.shape[:nd - 4] + (-1, t.shape[-1]))


def nsa_sample(h, pool_t, win_t, win_cache, layer, page_table, pe_k, pe_v, w1k, w2k, w1v, w2v):
    b, ts, _ = h.shape
    g, dh, nh = NSA_KV_GROUPS, NSA_DH, NSA_HPG
    assert NSA_QD % (4 * NSA_KV_DIM) == 0 and (NSA_QD + 4 * NSA_KV_DIM) % (2 * NSA_KV_DIM) == 0
    kv_new = h[:, :, NSA_QD:NSA_QD + 4 * NSA_KV_DIM].reshape(b, ts, 4, g, dh)
    win_new = h[:, :, NSA_QD + 4 * NSA_KV_DIM:NSA_QD + 6 * NSA_KV_DIM].reshape(b, ts, 2, g, dh)
    q_t = jnp.transpose(h[:, :, :NSA_QD].reshape(b, ts, g, nh, dh), (0, 2, 4, 3, 1)).reshape(b, g, dh, nh * ts)
    gate = h[:, :, NSA_QD + 6 * NSA_KV_DIM:NSA_IN].reshape(b, ts, g, nh, 3)
    gate_t = jnp.pad(jnp.transpose(gate, (0, 2, 4, 3, 1)).reshape(b, g, 3, nh * ts), ((0, 0), (0, 0), (0, 5), (0, 0)))
    pe = jnp.stack([pe_k, pe_v])
    w1 = jnp.stack([w1k, w1v]).reshape(2, CMP_LEN, dh, CMP_HID).astype(BF16)
    w2 = jnp.stack([w2k, w2v]).astype(BF16)
    cmp = nsa_compress_paged(pool_t, layer, page_table, pe, w1, w2)
    o_t = nsa_sample_attend(h, q_t, gate_t, cmp, pool_t, win_t, layer, page_table)
    o = jnp.transpose(o_t.reshape(b, g, dh, nh, ts), (0, 4, 1, 3, 2)).reshape(b, ts, NSA_QD)
    win_all = jnp.concatenate([win_cache, win_new], axis=1)
    return o, kv_new, win_all[:, ts:]


def _pad_cols(w, n):
    return jnp.pad(w, ((0, 0), (0, n - w.shape[1])))


def kernel(x_prompt, x_sample, cache_nsa_kv, cache_nsa_win, state_gdn, state_gdn_conv, page_table,
           norm1_w, norm2_w, final_norm_w, gdn_w_in, gdn_conv_w, gdn_A_log, gdn_dt_bias, gdn_norm_w, gdn_w_out,
           nsa_w_in, nsa_pe_k, nsa_pe_v, nsa_cmp_w1_k, nsa_cmp_w2_k, nsa_cmp_w1_v, nsa_cmp_w2_v, nsa_w_out,
           ffn_w_in, ffn_w_out):
    bp, tp, d = x_prompt.shape
    bs, ts, _ = x_sample.shape
    mp, ms = bp * tp, bs * ts
    win_buf_len = cache_nsa_win.shape[2]
    x = jnp.concatenate([x_prompt.reshape(mp, d), x_sample.reshape(ms, d)], axis=0)
    gdn_n = _round_up(GDN_IN, LANES)
    nsa_n = _round_up(NSA_IN, LANES)
    pool_t = _feature_major(cache_nsa_kv)
    win_t = _feature_major(cache_nsa_win)
    kv_p, kv_s, win_p, win_s = [], [], [], []
    gs_p, gs_s, gc_p, gc_s = [], [], [], []
    for i in range(DEPTH):
        li = i // N_MIXERS
        if i % N_MIXERS == 0:
            h = norm_proj(x, norm1_w[i], _pad_cols(gdn_w_in[li], gdn_n).astype(BF16))
            hp = h[:mp].reshape(bp, tp, gdn_n)
            hs = h[mp:].reshape(bs, ts, gdn_n)
            gw = (gdn_conv_w[li], gdn_A_log[li], gdn_dt_bias[li], gdn_norm_w[li])
            s0 = jnp.zeros((bp, GDN_HEADS, GDN_DK, GDN_DV), F32)
            c0 = jnp.zeros((bp, GDN_CONV - 1, GDN_QKV), F32)
            op, sp, cp = gdn_mixer(hp, s0, c0, min(GDN_CHUNK, tp), *gw)
            os_, ss, cs = gdn_mixer(hs, state_gdn[li], state_gdn_conv[li], ts, *gw)
            gs_p.append(sp)
            gs_s.append(ss)
            gc_p.append(cp)
            gc_s.append(cs)
            o = jnp.concatenate([op.reshape(mp, -1), os_.reshape(ms, -1)], axis=0)
            x = proj_residual(x, o, gdn_w_out[li].astype(BF16))
        else:
            h = norm_proj(x, norm1_w[i], _pad_cols(nsa_w_in[li], nsa_n).astype(BF16))
            hp = h[:mp].reshape(bp, tp, nsa_n)
            hs = h[mp:].reshape(bs, ts, nsa_n)
            nw = (nsa_pe_k[li], nsa_pe_v[li], nsa_cmp_w1_k[li], nsa_cmp_w2_k[li], nsa_cmp_w1_v[li], nsa_cmp_w2_v[li])
            op, kp, wp = nsa_prompt(hp, win_buf_len, *nw)
            os_, ksm, wsm = nsa_sample(hs, pool_t, win_t, cache_nsa_win[li], li, page_table, *nw)
            kv_p.append(kp)
            kv_s.append(ksm)
            win_p.append(wp)
            win_s.append(wsm)
            o = jnp.concatenate([op.reshape(mp, -1), os_.reshape(ms, -1)], axis=0)
            x = proj_residual(x, o, nsa_w_out[li].astype(BF16))
        x = ffn(x, norm2_w[i], ffn_w_in[i].astype(BF16), ffn_w_out[i].astype(BF16), final_norm_w, i == DEPTH - 1)
    y_prompt = x[:mp].reshape(bp, tp, d)
    y_sample = x[mp:].reshape(bs, ts, d)
    return (y_prompt, y_sample, jnp.stack(kv_p), jnp.stack(kv_s), jnp.stack(win_p), jnp.stack(win_s),
            jnp.stack(gs_p), jnp.stack(gs_s), jnp.stack(gc_p), jnp.stack(gc_s))
```
